```python
import math
import jax
import jax.numpy as jnp
from jax import lax
import numpy as np

D_MODEL = 2048
BATCH = 16
SEQ = 2048
DEPTH = 4
DEC_BATCH = 8
DEC_SEQ = 16
PAST_LEN = 4096

CHUNK = 64
D_MIX = D_MODEL
D_GROUP = D_MIX // 4
ATT_HEADS = 8
ATT_DH = D_GROUP // ATT_HEADS
BAND = 8
REL_CLIP = 128
ML_HEADS = 4
ML_DH = D_GROUP // ML_HEADS
CONV_W = 3
S5_GROUP_CH = 16
S5_GROUPS = D_GROUP // S5_GROUP_CH
S5_STATE = 64
NORM_EPS = 1e-6
SPLITS = (D_GROUP,) * 4 + (D_GROUP,) * 5 + (2 * ML_HEADS,) + (D_GROUP,) * 4 + (D_GROUP,) * 2
D_IN = 15 * D_GROUP + 2 * ML_HEADS

kernel_name = "hybrid_streaming_encoder_step"


def _rmsnorm(x, g):
    x32 = x.astype(jnp.float32)
    y = x32 * lax.rsqrt(jnp.mean(x32 * x32, axis=-1, keepdims=True) + NORM_EPS)
    return (y * g.astype(jnp.float32)).astype(x.dtype)


def _split(u):
    idx = np.cumsum(np.array(SPLITS))[:-1].tolist()
    return jnp.split(u, idx, axis=-1)


def _rel_bias(rel_bias, n_q, n_k, offset):
    rel = jnp.arange(n_q)[:, None] + offset - jnp.arange(n_k)[None, :]
    idx = jnp.clip(rel, -REL_CLIP, REL_CLIP) + REL_CLIP
    return rel_bias[:, idx].astype(jnp.float32)


def _band_attention_prompt(q, k, v, rel_bias):
    B, S, H, Dh = q.shape
    nc = S // CHUNK
    pad = BAND * CHUNK
    nk = (BAND + 1) * CHUNK
    qc = q.reshape(B, nc, CHUNK, H, Dh)
    kp = jnp.pad(k, ((0, 0), (pad, 0), (0, 0), (0, 0))).reshape(B, nc + BAND, CHUNK, H, Dh)
    vp = jnp.pad(v, ((0, 0), (pad, 0), (0, 0), (0, 0))).reshape(B, nc + BAND, CHUNK, H, Dh)
    kb = jnp.concatenate([kp[:, j:j + nc] for j in range(BAND + 1)], axis=2)
    vb = jnp.concatenate([vp[:, j:j + nc] for j in range(BAND + 1)], axis=2)
    s = jnp.einsum('bcqhd,bckhd->bchqk', qc, kb).astype(jnp.float32) * (Dh ** -0.5)
    s = s + _rel_bias(rel_bias, CHUNK, nk, pad)[None, None]
    kpos = (jnp.arange(nc)[:, None] - BAND) * CHUNK + jnp.arange(nk)[None, :]
    s = jnp.where((kpos >= 0)[None, :, None, None, :], s, -jnp.inf)
    p = jax.nn.softmax(s, axis=-1).astype(v.dtype)
    o = jnp.einsum('bchqk,bckhd->bcqhd', p, vb)
    return o.reshape(B, S, H * Dh)


def _band_attention_sample(q, k, v, k_cache, v_cache, rel_bias):
    B, T, H, Dh = q.shape
    W = k_cache.shape[1]
    k_all = jnp.concatenate([k_cache, k.astype(k_cache.dtype)], axis=1)
    v_all = jnp.concatenate([v_cache, v.astype(v_cache.dtype)], axis=1)
    s = jnp.einsum('bqhd,bkhd->bhqk', q, k_all).astype(jnp.float32) * (Dh ** -0.5)
    s = s + _rel_bias(rel_bias, T, W + T, W)[None]
    p = jax.nn.softmax(s, axis=-1).astype(v_all.dtype)
    o = jnp.einsum('bhqk,bkhd->bqhd', p, v_all).reshape(B, T, H * Dh)
    return o, k_all[:, T:], v_all[:, T:]


def _mlstm_chunk(carry, inp):
    C, n, m = carry
    q, k, v, ig, lf = inp
    L = q.shape[2]
    b = jnp.cumsum(lf, axis=-1)
    causal = jnp.tril(jnp.ones((L, L), dtype=bool))
    d = jnp.where(causal, b[..., :, None] - b[..., None, :] + ig[..., None, :], -jnp.inf)
    m_inter = b + m[..., None]
    m_t = jnp.maximum(m_inter, jnp.max(d, axis=-1))
    w_intra = jnp.exp(d - m_t[..., None]) * jnp.einsum('bhtd,bhsd->bhts', q, k)
    w_inter = jnp.exp(m_inter - m_t)
    num = w_inter[..., None] * jnp.einsum('bhvk,bhtk->bhtv', C, q) + jnp.einsum('bhts,bhsv->bhtv', w_intra, v)
    den = w_inter * jnp.einsum('bhk,bhtk->bht', n, q) + jnp.sum(w_intra, axis=-1)
    h = num / jnp.maximum(jnp.abs(den), jnp.exp(-m_t))[..., None]
    b_last = b[..., -1]
    m_new = jnp.maximum(b_last + m, jnp.max(b_last[..., None] - b + ig, axis=-1))
    ws = jnp.exp(b_last[..., None] - b + ig - m_new[..., None])
    decay = jnp.exp(b_last + m - m_new)
    C_new = decay[..., None, None] * C + jnp.einsum('bhs,bhsv,bhsk->bhvk', ws, v, k)
    n_new = decay[..., None] * n + jnp.einsum('bhs,bhsk->bhk', ws, k)
    return (C_new, n_new, m_new), h


def _mlstm(q, k, v, ig, lf, state):
    B, S, H, Dh = q.shape
    L = CHUNK if S % CHUNK == 0 else S
    nc = S // L

    def chunks(t):
        t = t.astype(jnp.float32).reshape((B, nc, L, H) + t.shape[3:])
        return jnp.moveaxis(t, (1, 3), (0, 2))

    state = tuple(s.astype(jnp.float32) for s in state)
    xs = (chunks(q), chunks(k) * (Dh ** -0.5), chunks(v), chunks(ig), chunks(lf))
    state, h = lax.scan(_mlstm_chunk, state, xs)
    h = jnp.moveaxis(h, (0, 2), (1, 3)).reshape(B, S, H * Dh)
    return h, state


def _short_conv(bg, cg, xin, buf, w):
    S = xin.shape[1]
    u = jnp.concatenate([buf.astype(xin.dtype), cg * xin], axis=1)
    y = sum(w[j] * u[:, j:j + S] for j in range(CONV_W))
    return bg * y, u[:, S:]


def _ssm_combine(e1, e2):
    a1r, a1i, b1r, b1i = e1
    a2r, a2i, b2r, b2i = e2
    return (a2r * a1r - a2i * a1i,
            a2r * a1i + a2i * a1r,
            a2r * b1r - a2i * b1i + b2r,
            a2r * b1i + a2i * b1r + b2i)


def _s5(u, s0_re, s0_im, A_re, A_im, log_dt, B_re, B_im, C_re, C_im, D_skip, w_glu, b_glu):
    Bsz, S, _ = u.shape
    f32 = jnp.float32
    u32 = u.astype(f32)
    ug = u32.reshape(Bsz, S, S5_GROUPS, S5_GROUP_CH)
    A_re, A_im = A_re.astype(f32), A_im.astype(f32)
    dt = jnp.exp(log_dt.astype(f32))[:, None]
    mag = jnp.exp(A_re * dt)
    ab_re, ab_im = mag * jnp.cos(A_im * dt), mag * jnp.sin(A_im * dt)
    den = A_re * A_re + A_im * A_im
    co_re = ((ab_re - 1.0) * A_re + ab_im * A_im) / den
    co_im = (ab_im * A_re - (ab_re - 1.0) * A_im) / den
    B_re, B_im = B_re.astype(f32), B_im.astype(f32)
    bb_re = co_re[..., None] * B_re - co_im[..., None] * B_im
    bb_im = co_re[..., None] * B_im + co_im[..., None] * B_re
    bu_re = jnp.einsum('gpc,bsgc->bsgp', bb_re, ug)
    bu_im = jnp.einsum('gpc,bsgc->bsgp', bb_im, ug)
    s0_re, s0_im = s0_re.astype(f32), s0_im.astype(f32)
    bu_re = bu_re.at[:, 0].add(ab_re * s0_re - ab_im * s0_im)
    bu_im = bu_im.at[:, 0].add(ab_re * s0_im + ab_im * s0_re)
    a_re = jnp.broadcast_to(ab_re, bu_re.shape)
    a_im = jnp.broadcast_to(ab_im, bu_im.shape)
    _, _, x_re, x_im = lax.associative_scan(_ssm_combine, (a_re, a_im, bu_re, bu_im), axis=1)
    y = (jnp.einsum('gcp,bsgp->bsgc', C_re.astype(f32), x_re)
         - jnp.einsum('gcp,bsgp->bsgc', C_im.astype(f32), x_im))
    y = y.reshape(Bsz, S, D_GROUP) + D_skip.astype(f32) * u32
    y = jax.nn.gelu(y)
    y = y * jax.nn.sigmoid(y @ w_glu.astype(f32) + b_glu.astype(f32))
    return y, x_re[:, -1], x_im[:, -1]


def _layer(x, att_cache, ml_state, conv_buf, s5_state, p):
    (g, w_in, w_out, rel_bias, b_if, conv_w, A_re, A_im, log_dt,
     B_re, B_im, C_re, C_im, D_skip, w_glu, b_glu) = p
    Bsz, S, _ = x.shape
    h = _rmsnorm(x, g)
    (aq, ak, av, az, mq, mk, mv, mo, mz, mif, cb, cc, cx, cz, su, sz) = _split(h @ w_in)

    def heads(t, n_h):
        return t.reshape(Bsz, S, n_h, -1)

    aq, ak, av = heads(aq, ATT_HEADS), heads(ak, ATT_HEADS), heads(av, ATT_HEADS)
    if att_cache is None:
        ya = _band_attention_prompt(aq, ak, av, rel_bias)
        rows = min(BAND * CHUNK, S)
        new_k, new_v = ak[:, S - rows:], av[:, S - rows:]
    else:
        ya, new_k, new_v = _band_attention_sample(aq, ak, av, att_cache[0], att_cache[1], rel_bias)
    gates = mif.astype(jnp.float32) + b_if.astype(jnp.float32)
    ym, (C_new, n_new, m_new) = _mlstm(heads(mq, ML_HEADS), heads(mk, ML_HEADS), heads(mv, ML_HEADS),
                                      gates[..., :ML_HEADS], jax.nn.log_sigmoid(gates[..., ML_HEADS:]),
                                      ml_state)
    ym = jax.nn.sigmoid(mo.astype(jnp.float32)) * ym
    yc, conv_new = _short_conv(cb, cc, cx, conv_buf, conv_w)
    ys, s5_re, s5_im = _s5(su, s5_state[0], s5_state[1], A_re, A_im, log_dt,
                           B_re, B_im, C_re, C_im, D_skip, w_glu, b_glu)
    branches = ((ya, az), (ym, mz), (yc, cz), (ys, sz))
    y = jnp.concatenate([(jax.nn.silu(z.astype(jnp.float32)) * yb.astype(jnp.float32)).astype(x.dtype)
                         for yb, z in branches], axis=-1)
    x = x + (y @ w_out).astype(x.dtype)
    return x, (new_k, new_v, C_new, n_new, m_new, conv_new, s5_re, s5_im)


def _stack(states):
    return [jnp.stack(t) for t in zip(*states)]


def setup_inputs(seed: int = 0) -> dict:
    key = jax.random.key(seed)
    ks = jax.random.split(key, 32)
    f32 = jnp.float32

    def nrm(k, shape, scale):
        return jax.random.normal(k, shape, f32) * scale

    att_rows = min(BAND * CHUNK, PAST_LEN)
    b_if = jnp.concatenate([-1.0 + nrm(ks[20], (DEPTH, ML_HEADS), 0.1),
                            3.0 + nrm(ks[21], (DEPTH, ML_HEADS), 0.5)], axis=-1)
    a_im = math.pi * jnp.arange(S5_STATE, dtype=f32)[None, None, :] + nrm(ks[22], (DEPTH, S5_GROUPS, S5_STATE), 0.01)
    log_dt = jax.random.uniform(ks[23], (DEPTH, S5_GROUPS), f32, math.log(0.001), math.log(0.1))
    return {
        "x_prompt": nrm(ks[0], (BATCH, SEQ, D_MODEL), 1.0),
        "x_sample": nrm(ks[1], (DEC_BATCH, DEC_SEQ, D_MODEL), 1.0),
        "cache_attn_k": nrm(ks[2], (DEPTH, DEC_BATCH, att_rows, ATT_HEADS, ATT_DH), 1.0),
        "cache_attn_v": nrm(ks[3], (DEPTH, DEC_BATCH, att_rows, ATT_HEADS, ATT_DH), 1.0),
        "state_mlstm_C": nrm(ks[4], (DEPTH, DEC_BATCH, ML_HEADS, ML_DH, ML_DH), 0.1),
        "state_mlstm_n": nrm(ks[5], (DEPTH, DEC_BATCH, ML_HEADS, ML_DH), 0.1),
        "state_mlstm_m": nrm(ks[6], (DEPTH, DEC_BATCH, ML_HEADS), 0.5),
        "state_conv": nrm(ks[7], (DEPTH, DEC_BATCH, CONV_W - 1, D_GROUP), 1.0),
        "state_s5_re": nrm(ks[8], (DEPTH, DEC_BATCH, S5_GROUPS, S5_STATE), 0.5),
        "state_s5_im": nrm(ks[9], (DEPTH, DEC_BATCH, S5_GROUPS, S5_STATE), 0.5),
        "norm_g": 1.0 + nrm(ks[10], (DEPTH, D_MODEL), 0.02),
        "w_in": nrm(ks[11], (DEPTH, D_MODEL, D_IN), D_MODEL ** -0.5),
        "w_out": nrm(ks[12], (DEPTH, D_MIX, D_MODEL), 0.5 * D_MIX ** -0.5),
        "attn_rel_bias": nrm(ks[13], (DEPTH, ATT_HEADS, 2 * REL_CLIP + 1), 0.1),
        "mlstm_b_if": b_if,
        "conv_w": nrm(ks[14], (DEPTH, CONV_W, D_GROUP), CONV_W ** -0.5),
        "s5_A_re": -0.5 + nrm(ks[15], (DEPTH, S5_GROUPS, S5_STATE), 0.01),
        "s5_A_im": a_im,
        "s5_log_dt": log_dt,
        "s5_B_re": nrm(ks[16], (DEPTH, S5_GROUPS, S5_STATE, S5_GROUP_CH), (2 * S5_GROUP_CH) ** -0.5),
        "s5_B_im": nrm(ks[17], (DEPTH, S5_GROUPS, S5_STATE, S5_GROUP_CH), (2 * S5_GROUP_CH) ** -0.5),
        "s5_C_re": nrm(ks[18], (DEPTH, S5_GROUPS, S5_GROUP_CH, S5_STATE), (2 * S5_STATE) ** -0.5),
        "s5_C_im": nrm(ks[19], (DEPTH, S5_GROUPS, S5_GROUP_CH, S5_STATE), (2 * S5_STATE) ** -0.5),
        "s5_D": nrm(ks[24], (DEPTH, D_GROUP), 0.5),
        "s5_w_glu": nrm(ks[25], (DEPTH, D_GROUP, D_GROUP), D_GROUP ** -0.5),
        "s5_b_glu": nrm(ks[26], (DEPTH, D_GROUP), 0.01),
        "final_norm_g": 1.0 + nrm(ks[27], (D_MODEL,), 0.02),
    }


def reference(x_prompt, x_sample, cache_attn_k, cache_attn_v, state_mlstm_C, state_mlstm_n,
              state_mlstm_m, state_conv, state_s5_re, state_s5_im, norm_g, w_in, w_out,
              attn_rel_bias, mlstm_b_if, conv_w, s5_A_re, s5_A_im, s5_log_dt, s5_B_re, s5_B_im,
              s5_C_re, s5_C_im, s5_D, s5_w_glu, s5_b_glu, final_norm_g):
    f32 = jnp.float32
    Bp = x_prompt.shape[0]
    ml0 = (jnp.zeros((Bp, ML_HEADS, ML_DH, ML_DH), f32), jnp.zeros((Bp, ML_HEADS, ML_DH), f32),
           jnp.zeros((Bp, ML_HEADS), f32))
    conv0 = jnp.zeros((Bp, CONV_W - 1, D_GROUP), x_prompt.dtype)
    s50 = (jnp.zeros((Bp, S5_GROUPS, S5_STATE), f32), jnp.zeros((Bp, S5_GROUPS, S5_STATE), f32))
    xp, xs = x_prompt, x_sample
    st_p, st_s = [], []
    for l in range(DEPTH):
        p = (norm_g[l], w_in[l], w_out[l], attn_rel_bias[l], mlstm_b_if[l], conv_w[l],
             s5_A_re[l], s5_A_im[l], s5_log_dt[l], s5_B_re[l], s5_B_im[l], s5_C_re[l], s5_C_im[l],
             s5_D[l], s5_w_glu[l], s5_b_glu[l])
        xp, sp = _layer(xp, None, ml0, conv0, s50, p)
        xs, ss = _layer(xs, (cache_attn_k[l], cache_attn_v[l]),
                        (state_mlstm_C[l], state_mlstm_n[l], state_mlstm_m[l]),
                        state_conv[l], (state_s5_re[l], state_s5_im[l]), p)
        st_p.append(sp)
        st_s.append(ss)
    y_prompt = _rmsnorm(xp, final_norm_g)
    y_sample = _rmsnorm(xs, final_norm_g)
    (pk, pv, pC, pn, pm, pconv, pre, pim) = _stack(st_p)
    (sk, sv, sC, sn, sm, sconv, sre, sim) = _stack(st_s)
    return (y_prompt, y_sample, pk, pv, pC, pn, pm, pconv, pre, pim,
            sk, sv, sC, sn, sm, sconv, sre, sim)
```

```python
import functools
import math

import jax
import jax.numpy as jnp
from jax import lax
from jax.experimental import pallas as pl
from jax.experimental.pallas import tpu as pltpu

F32 = jnp.float32
BF16 = jnp.bfloat16

D_MODEL = 2048
D_GROUP = 512
CHUNK = 64
BAND = 8
REL_CLIP = 128
ATT_HEADS = 8
ATT_DH = 64
ML_HEADS = 4
ML_DH = 128
CONV_W = 3
S5_GROUPS = 32
S5_CH = 16
S5_STATE = 64
S5_BLOCKS = 4
S5_BLK_STATES = S5_GROUPS * S5_STATE // S5_BLOCKS
NORM_EPS = 1e-6
N_MAIN = 15 * D_GROUP
GATE_PAD = 128
SUBLANES = 8
VMEM_LIMIT = 56 * 1024 * 1024

COL_AQ, COL_AK, COL_AV, COL_AZ = 0, 1, 2, 3
COL_MQ, COL_MK, COL_MV, COL_MO, COL_MZ = 4, 5, 6, 7, 8
COL_CB, COL_CC, COL_CX, COL_CZ = 9, 10, 11, 12
COL_SU, COL_SZ = 13, 14


def _params(n_axes):
    return pltpu.CompilerParams(dimension_semantics=("arbitrary",) * n_axes,
                                vmem_limit_bytes=VMEM_LIMIT)


def _sigmoid(x):
    return 1.0 / (1.0 + jnp.exp(-x))


def _silu(x):
    return x * _sigmoid(x)


def _log_sigmoid(x):
    return jnp.minimum(x, 0.0) - jnp.log(1.0 + jnp.exp(-jnp.abs(x)))


def _gelu_tanh(x):
    return 0.5 * x * (1.0 + jnp.tanh(math.sqrt(2.0 / math.pi) * (x + 0.044715 * (x * x * x))))


def _rms(x, g):
    return x * lax.rsqrt(jnp.mean(x * x, axis=-1, keepdims=True) + NORM_EPS) * g


def _dot(a, b):
    return jnp.dot(a, b, preferred_element_type=F32)


def _dot_nt(a, b):
    return lax.dot_general(a, b, (((1,), (1,)), ((), ())), preferred_element_type=F32)


def _dot_tn(a, b):
    return lax.dot_general(a, b, (((0,), (0,)), ((), ())), preferred_element_type=F32)


def _dot_exact(a, b):
    return jnp.dot(a, b, preferred_element_type=F32, precision=lax.Precision.HIGHEST)


def _inproj_kernel(x_ref, g_ref, w_ref, wg_ref, u_ref, gate_ref, xn_ref):
    @pl.when(pl.program_id(1) == 0)
    def _():
        xn_ref[...] = _rms(x_ref[...], g_ref[...]).astype(BF16)
        gate_ref[...] = _dot(xn_ref[...], wg_ref[...])

    u_ref[...] = _dot(xn_ref[...], w_ref[...]).astype(u_ref.dtype)


def _inproj(x, g, w_main, w_gate, tm, tn):
    n = x.shape[0]
    return pl.pallas_call(
        _inproj_kernel,
        grid=(n // tm, N_MAIN // tn),
        in_specs=[
            pl.BlockSpec((tm, D_MODEL), lambda i, j: (i, 0)),
            pl.BlockSpec((1, D_MODEL), lambda i, j: (0, 0)),
            pl.BlockSpec((D_MODEL, tn), lambda i, j: (0, j)),
            pl.BlockSpec((D_MODEL, GATE_PAD), lambda i, j: (0, 0)),
        ],
        out_specs=[
            pl.BlockSpec((tm, tn), lambda i, j: (i, j)),
            pl.BlockSpec((tm, GATE_PAD), lambda i, j: (i, 0)),
        ],
        out_shape=[
            jax.ShapeDtypeStruct((n, N_MAIN), F32),
            jax.ShapeDtypeStruct((n, GATE_PAD), F32),
        ],
        scratch_shapes=[pltpu.VMEM((tm, D_MODEL), BF16)],
        compiler_params=_params(2),
        name="inproj",
    )(x, g, w_main, w_gate)


def _outproj_kernel(ya_ref, ym_ref, yc_ref, ys_ref, x_ref, w_ref, o_ref):
    acc = x_ref[...]
    for k, y_ref in enumerate((ya_ref, ym_ref, yc_ref, ys_ref)):
        acc = acc + _dot(y_ref[...], w_ref[k * D_GROUP:(k + 1) * D_GROUP, :])
    o_ref[...] = acc


def _outproj(ya, ym, yc, ys, x, w_out, tm):
    n = x.shape[0]
    yspec = pl.BlockSpec((tm, D_GROUP), lambda i: (i, 0))
    return pl.pallas_call(
        _outproj_kernel,
        grid=(n // tm,),
        in_specs=[yspec, yspec, yspec, yspec,
                  pl.BlockSpec((tm, D_MODEL), lambda i: (i, 0)),
                  pl.BlockSpec((D_MODEL, D_MODEL), lambda i: (0, 0))],
        out_specs=pl.BlockSpec((tm, D_MODEL), lambda i: (i, 0)),
        out_shape=jax.ShapeDtypeStruct((n, D_MODEL), F32),
        compiler_params=_params(1),
        name="outproj",
    )(ya, ym, yc, ys, x, w_out)


def _final_norm_kernel(x_ref, g_ref, o_ref):
    o_ref[...] = _rms(x_ref[...], g_ref[...])


def _final_norm(x, g, tm):
    n = x.shape[0]
    return pl.pallas_call(
        _final_norm_kernel,
        grid=(n // tm,),
        in_specs=[pl.BlockSpec((tm, D_MODEL), lambda i: (i, 0)),
                  pl.BlockSpec((1, D_MODEL), lambda i: (0, 0))],
        out_specs=pl.BlockSpec((tm, D_MODEL), lambda i: (i, 0)),
        out_shape=jax.ShapeDtypeStruct((n, D_MODEL), F32),
        compiler_params=_params(1),
        name="final_norm",
    )(x, g)


def _attn_heads(q, z, k_all, v_all, bias_ref, valid, o_ref):
    for h in range(ATT_HEADS):
        sl = slice(h * ATT_DH, (h + 1) * ATT_DH)
        qh = (q[:, sl] * (ATT_DH ** -0.5)).astype(BF16)
        s = _dot_nt(qh, k_all[:, sl]) + bias_ref[h]
        if valid is not None:
            s = jnp.where(valid, s, -jnp.inf)
        p = jnp.exp(s - jnp.max(s, axis=-1, keepdims=True))
        o = _dot(p.astype(BF16), v_all[:, sl]) / jnp.sum(p, axis=-1, keepdims=True)
        o_ref[:, sl] = (_silu(z[:, sl]) * o).astype(o_ref.dtype)


def _attn_prompt_kernel(q_ref, k_ref, v_ref, z_ref, bias_ref, o_ref, kpad_ref, vpad_ref):
    c = pl.program_id(1)
    pad = BAND * CHUNK
    nk = (BAND + 1) * CHUNK

    @pl.when(c == 0)
    def _():
        kpad_ref[0:pad, :] = jnp.zeros((pad, D_GROUP), BF16)
        vpad_ref[0:pad, :] = jnp.zeros((pad, D_GROUP), BF16)
        kpad_ref[pad:, :] = k_ref[...].astype(BF16)
        vpad_ref[pad:, :] = v_ref[...].astype(BF16)

    start = pl.multiple_of(c * CHUNK, CHUNK)
    k_all = kpad_ref[pl.ds(start, nk), :]
    v_all = vpad_ref[pl.ds(start, nk), :]
    kcol = lax.broadcasted_iota(jnp.int32, (CHUNK, nk), 1)
    valid = kcol >= (BAND - c) * CHUNK
    _attn_heads(q_ref[...], z_ref[...], k_all, v_all, bias_ref, valid, o_ref)


def _attn_prompt(u, bias, batch, seq):
    nc = seq // CHUNK
    nk = (BAND + 1) * CHUNK
    return pl.pallas_call(
        _attn_prompt_kernel,
        grid=(batch, nc),
        in_specs=[
            pl.BlockSpec((CHUNK, D_GROUP), lambda b, c: (b * nc + c, COL_AQ)),
            pl.BlockSpec((seq, D_GROUP), lambda b, c: (b, COL_AK)),
            pl.BlockSpec((seq, D_GROUP), lambda b, c: (b, COL_AV)),
            pl.BlockSpec((CHUNK, D_GROUP), lambda b, c: (b * nc + c, COL_AZ)),
            pl.BlockSpec((ATT_HEADS, CHUNK, nk), lambda b, c: (0, 0, 0)),
        ],
        out_specs=pl.BlockSpec((CHUNK, D_GROUP), lambda b, c: (b * nc + c, 0)),
        out_shape=jax.ShapeDtypeStruct((batch * seq, D_GROUP), BF16),
        scratch_shapes=[pltpu.VMEM((seq + BAND * CHUNK, D_GROUP), BF16),
                        pltpu.VMEM((seq + BAND * CHUNK, D_GROUP), BF16)],
        compiler_params=_params(2),
        name="attn_prompt",
    )(u, u, u, u, bias)


def _attn_sample_kernel(q_ref, k_ref, v_ref, z_ref, kc_ref, vc_ref, bias_ref,
                        o_ref, kn_ref, vn_ref, kall_ref, vall_ref):
    w = kc_ref.shape[1]
    t = q_ref.shape[0]
    kall_ref[0:w, :] = kc_ref[0].astype(BF16)
    vall_ref[0:w, :] = vc_ref[0].astype(BF16)
    kall_ref[w:, :] = k_ref[...].astype(BF16)
    vall_ref[w:, :] = v_ref[...].astype(BF16)
    kn_ref[0, 0:w - t, :] = kc_ref[0, t:, :]
    vn_ref[0, 0:w - t, :] = vc_ref[0, t:, :]
    kn_ref[0, w - t:, :] = k_ref[...]
    vn_ref[0, w - t:, :] = v_ref[...]
    _attn_heads(q_ref[...], z_ref[...], kall_ref[...], vall_ref[...], bias_ref, None, o_ref)


def _attn_sample(u, k_cache, v_cache, bias, batch, seq):
    w = k_cache.shape[1]
    tok = pl.BlockSpec((seq, D_GROUP), lambda b: (b, 0))
    cache = pl.BlockSpec((1, w, D_GROUP), lambda b: (b, 0, 0))
    return pl.pallas_call(
        _attn_sample_kernel,
        grid=(batch,),
        in_specs=[
            pl.BlockSpec((seq, D_GROUP), lambda b: (b, COL_AQ)),
            pl.BlockSpec((seq, D_GROUP), lambda b: (b, COL_AK)),
            pl.BlockSpec((seq, D_GROUP), lambda b: (b, COL_AV)),
            pl.BlockSpec((seq, D_GROUP), lambda b: (b, COL_AZ)),
            cache, cache,
            pl.BlockSpec((ATT_HEADS, seq, w + seq), lambda b: (0, 0, 0)),
        ],
        out_specs=[tok, cache, cache],
        out_shape=[jax.ShapeDtypeStruct((batch * seq, D_GROUP), BF16),
                   jax.ShapeDtypeStruct(k_cache.shape, F32),
                   jax.ShapeDtypeStruct(v_cache.shape, F32)],
        scratch_shapes=[pltpu.VMEM((w + seq, D_GROUP), BF16),
                        pltpu.VMEM((w + seq, D_GROUP), BF16)],
        compiler_params=_params(1),
        name="attn_sample",
    )(u, u, u, u, k_cache, v_cache, bias)


def _mlstm_kernel(q_ref, k_ref, v_ref, o_ref, z_ref, gc_ref, gr_ref, bc_ref, br_ref,
                  c0_ref, n0_ref, m0_ref,
                  y_ref, cout_ref, nout_ref, mout_ref, c_s, n_s, m_s):
    c = pl.program_id(1)
    L = q_ref.shape[0]

    @pl.when(c == 0)
    def _():
        c_s[...] = c0_ref[0]
        n_s[...] = n0_ref[0]
        m_s[...] = m0_ref[0]

    row = lax.broadcasted_iota(jnp.int32, (L, L), 0)
    col = lax.broadcasted_iota(jnp.int32, (L, L), 1)
    tril = (row >= col).astype(F32)
    triu = (row <= col).astype(F32)
    causal = row >= col

    g_col = gc_ref[...] + bc_ref[...]
    g_row = gr_ref[0, 0] + br_ref[...]
    b_col = _dot_exact(tril, _log_sigmoid(g_col))
    b_row = _dot_exact(_log_sigmoid(g_row), triu)

    for h in range(ML_HEADS):
        sl = slice(h * ML_DH, (h + 1) * ML_DH)
        bt = b_col[:, ML_HEADS + h:ML_HEADS + h + 1]
        ig_t = g_col[:, h:h + 1]
        bs = b_row[ML_HEADS + h:ML_HEADS + h + 1, :]
        ig_s = g_row[h:h + 1, :]
        m_prev = m_s[h:h + 1, 0:1]
        n_prev = n_s[h:h + 1, :]
        c_prev = c_s[h]

        qh = q_ref[:, sl]
        kh = k_ref[:, sl] * (ML_DH ** -0.5)
        vh = v_ref[:, sl]
        qb, kb = qh.astype(BF16), kh.astype(BF16)

        d = jnp.where(causal, bt - bs + ig_s, -jnp.inf)
        m_inter = bt + m_prev
        m_t = jnp.maximum(m_inter, jnp.max(d, axis=-1, keepdims=True))
        w_intra = jnp.exp(d - m_t) * _dot_nt(qb, kb)
        w_inter = jnp.exp(m_inter - m_t)
        num = w_inter * _dot_nt(qb, c_prev.astype(BF16)) + _dot(w_intra.astype(BF16), vh.astype(BF16))
        den = (w_inter * jnp.sum(qh * n_prev, axis=-1, keepdims=True)
               + jnp.sum(w_intra, axis=-1, keepdims=True))
        hh = num / jnp.maximum(jnp.abs(den), jnp.exp(-m_t))
        ym = _sigmoid(o_ref[:, sl]) * hh
        y_ref[:, sl] = (_silu(z_ref[:, sl]) * ym).astype(y_ref.dtype)

        b_last = bt[L - 1:L, :]
        m_new = jnp.maximum(b_last + m_prev, jnp.max(b_last - bs + ig_s, axis=-1, keepdims=True))
        ws = jnp.exp(b_last - bt + ig_t - m_new)
        decay = jnp.exp(b_last + m_prev - m_new)
        c_s[h] = decay * c_prev + _dot_tn((vh * ws).astype(BF16), kb)
        n_s[h:h + 1, :] = decay * n_prev + jnp.sum(ws * kh, axis=0, keepdims=True)
        m_s[h:h + 1, :] = jnp.broadcast_to(m_new, (1, m_s.shape[1]))

    cout_ref[0] = c_s[...]
    nout_ref[0] = n_s[...]
    mout_ref[0] = m_s[...]


def _mlstm(u, gates, gates_t, b_col, b_row, c0, n0, m0, batch, seq, L):
    nc = seq // L

    def tok(colblk):
        return pl.BlockSpec((L, D_GROUP), lambda b, c: (b * nc + c, colblk))

    st_c = pl.BlockSpec((1, ML_HEADS, ML_DH, ML_DH), lambda b, c: (b, 0, 0, 0))
    st_n = pl.BlockSpec((1, ML_HEADS, ML_DH), lambda b, c: (b, 0, 0))
    st_m = pl.BlockSpec((1, SUBLANES, ML_DH), lambda b, c: (b, 0, 0))
    return pl.pallas_call(
        _mlstm_kernel,
        grid=(batch, nc),
        in_specs=[
            tok(COL_MQ), tok(COL_MK), tok(COL_MV), tok(COL_MO), tok(COL_MZ),
            pl.BlockSpec((L, GATE_PAD), lambda b, c: (b * nc + c, 0)),
            pl.BlockSpec((1, 1, SUBLANES, L), lambda b, c: (b, c, 0, 0)),
            pl.BlockSpec((1, GATE_PAD), lambda b, c: (0, 0)),
            pl.BlockSpec((SUBLANES, 1), lambda b, c: (0, 0)),
            st_c, st_n, st_m,
        ],
        out_specs=[pl.BlockSpec((L, D_GROUP), lambda b, c: (b * nc + c, 0)), st_c, st_n, st_m],
        out_shape=[jax.ShapeDtypeStruct((batch * seq, D_GROUP), BF16),
                   jax.ShapeDtypeStruct((batch, ML_HEADS, ML_DH, ML_DH), F32),
                   jax.ShapeDtypeStruct((batch, ML_HEADS, ML_DH), F32),
                   jax.ShapeDtypeStruct((batch, SUBLANES, ML_DH), F32)],
        scratch_shapes=[pltpu.VMEM((ML_HEADS, ML_DH, ML_DH), F32),
                        pltpu.VMEM((ML_HEADS, ML_DH), F32),
                        pltpu.VMEM((SUBLANES, ML_DH), F32)],
        compiler_params=_params(2),
        name="mlstm",
    )(u, u, u, u, u, gates, gates_t, b_col, b_row, c0, n0, m0)


def _conv_kernel(bg_ref, cg_ref, x_ref, z_ref, w_ref, s0_ref, y_ref, sout_ref, u_s):
    t = pl.program_id(1)
    T = x_ref.shape[0]
    head = SUBLANES

    @pl.when(t == 0)
    def _():
        u_s[head - (CONV_W - 1):head, :] = s0_ref[0]

    u_s[head:, :] = cg_ref[...] * x_ref[...]
    y = w_ref[0:1, :] * u_s[head - 2:head - 2 + T, :]
    y = y + w_ref[1:2, :] * u_s[head - 1:head - 1 + T, :]
    y = y + w_ref[2:3, :] * u_s[head:, :]
    y_ref[...] = (_silu(z_ref[...]) * (bg_ref[...] * y)).astype(y_ref.dtype)
    last = u_s[head + T - (CONV_W - 1):, :]
    sout_ref[0] = last
    u_s[head - (CONV_W - 1):head, :] = last


def _conv(u, w, s0, batch, seq, T):
    nt = seq // T

    def tok(colblk):
        return pl.BlockSpec((T, D_GROUP), lambda b, t: (b * nt + t, colblk))

    st = pl.BlockSpec((1, CONV_W - 1, D_GROUP), lambda b, t: (b, 0, 0))
    return pl.pallas_call(
        _conv_kernel,
        grid=(batch, nt),
        in_specs=[tok(COL_CB), tok(COL_CC), tok(COL_CX), tok(COL_CZ),
                  pl.BlockSpec((CONV_W, D_GROUP), lambda b, t: (0, 0)), st],
        out_specs=[pl.BlockSpec((T, D_GROUP), lambda b, t: (b * nt + t, 0)), st],
        out_shape=[jax.ShapeDtypeStruct((batch * seq, D_GROUP), BF16),
                   jax.ShapeDtypeStruct((batch, CONV_W - 1, D_GROUP), F32)],
        scratch_shapes=[pltpu.VMEM((SUBLANES + T, D_GROUP), F32)],
        compiler_params=_params(2),
        name="conv",
    )(u, u, u, u, w, s0)


def _s5_kernel(u_ref, z_ref, bd_ref, cd_ref, pre_ref, pim_ref, dskip_ref, wglu_ref, bglu_ref,
               s0re_ref, s0im_ref, y_ref, sre_ref, sim_ref, xr_s, xi_s, car_s, cai_s, y_s):
    t = pl.program_id(1)
    T = u_ref.shape[0]
    NS = S5_BLK_STATES

    @pl.when(t == 0)
    def _():
        car_s[...] = jnp.broadcast_to(s0re_ref[0], car_s.shape)
        cai_s[...] = jnp.broadcast_to(s0im_ref[0], cai_s.shape)

    u = u_ref[...]
    ub = u.astype(BF16)
    sub = lax.broadcasted_iota(jnp.int32, (T, NS), 0) % SUBLANES

    for blk in range(S5_BLOCKS):
        cols = slice(blk * NS, (blk + 1) * NS)
        bu = _dot(ub[:, blk * 128:(blk + 1) * 128], bd_ref[blk])
        xr, xi = bu[:, :NS], bu[:, NS:]
        for o in (1, 2, 4):
            ar = pre_ref[o - 1:o, cols]
            ai = pim_ref[o - 1:o, cols]
            keep = sub >= o
            sr = jnp.where(keep, pltpu.roll(xr, o, 0), 0.0)
            si = jnp.where(keep, pltpu.roll(xi, o, 0), 0.0)
            xr, xi = xr + (ar * sr - ai * si), xi + (ar * si + ai * sr)
        xr_s[...] = xr
        xi_s[...] = xi

        pr = pre_ref[:, cols]
        pi = pim_ref[:, cols]

        def body(g, carry):
            cr, ci = carry
            rows = pl.ds(pl.multiple_of(g * SUBLANES, SUBLANES), SUBLANES)
            gr = xr_s[rows, :] + (pr * cr - pi * ci)
            gi = xi_s[rows, :] + (pr * ci + pi * cr)
            xr_s[rows, :] = gr
            xi_s[rows, :] = gi
            return (jnp.broadcast_to(gr[SUBLANES - 1:, :], (SUBLANES, NS)),
                    jnp.broadcast_to(gi[SUBLANES - 1:, :], (SUBLANES, NS)))

        cr, ci = lax.fori_loop(0, T // SUBLANES, body, (car_s[:, cols], cai_s[:, cols]))
        car_s[:, cols] = cr
        cai_s[:, cols] = ci
        y_s[:, blk * 128:(blk + 1) * 128] = (_dot(xr_s[...].astype(BF16), cd_ref[blk, :NS, :])
                                             + _dot(xi_s[...].astype(BF16), cd_ref[blk, NS:, :]))

    y = _gelu_tanh(y_s[...] + dskip_ref[...] * u)
    y = y * _sigmoid(_dot(y.astype(BF16), wglu_ref[...]) + bglu_ref[...])
    y_ref[...] = (_silu(z_ref[...]) * y).astype(y_ref.dtype)
    sre_ref[0] = car_s[0:1, :]
    sim_ref[0] = cai_s[0:1, :]


def _s5(u, bd, cd, pw_re, pw_im, dskip, wglu, bglu, s0_re, s0_im, batch, seq, T):
    nt = seq // T
    ns_all = S5_GROUPS * S5_STATE

    def tok(colblk):
        return pl.BlockSpec((T, D_GROUP), lambda b, t: (b * nt + t, colblk))

    def full(shape):
        return pl.BlockSpec(shape, lambda b, t: (0,) * len(shape))

    st = pl.BlockSpec((1, 1, ns_all), lambda b, t: (b, 0, 0))
    return pl.pallas_call(
        _s5_kernel,
        grid=(batch, nt),
        in_specs=[tok(COL_SU), tok(COL_SZ),
                  full(bd.shape), full(cd.shape), full(pw_re.shape), full(pw_im.shape),
                  full(dskip.shape), full(wglu.shape), full(bglu.shape), st, st],
        out_specs=[pl.BlockSpec((T, D_GROUP), lambda b, t: (b * nt + t, 0)), st, st],
        out_shape=[jax.ShapeDtypeStruct((batch * seq, D_GROUP), BF16),
                   jax.ShapeDtypeStruct((batch, 1, ns_all), F32),
                   jax.ShapeDtypeStruct((batch, 1, ns_all), F32)],
        scratch_shapes=[pltpu.VMEM((T, S5_BLK_STATES), F32), pltpu.VMEM((T, S5_BLK_STATES), F32),
                        pltpu.VMEM((SUBLANES, ns_all), F32), pltpu.VMEM((SUBLANES, ns_all), F32),
                        pltpu.VMEM((T, D_GROUP), F32)],
        compiler_params=_params(2),
        name="s5",
    )(u, u, bd, cd, pw_re, pw_im, dskip, wglu, bglu, s0_re, s0_im)


def _rel_bias_table(rel_bias, n_q, n_k, offset):
    rel = jnp.arange(n_q)[:, None] + offset - jnp.arange(n_k)[None, :]
    idx = jnp.clip(rel, -REL_CLIP, REL_CLIP) + REL_CLIP
    return rel_bias[:, idx].astype(F32)


def _s5_params(a_re, a_im, log_dt, b_re, b_im, c_re, c_im):
    dt = jnp.exp(log_dt)[:, None]
    mag = jnp.exp(a_re * dt)
    ab_re, ab_im = mag * jnp.cos(a_im * dt), mag * jnp.sin(a_im * dt)
    den = a_re * a_re + a_im * a_im
    co_re = ((ab_re - 1.0) * a_re + ab_im * a_im) / den
    co_im = (ab_im * a_re - (ab_re - 1.0) * a_im) / den
    bb_re = co_re[..., None] * b_re - co_im[..., None] * b_im
    bb_im = co_re[..., None] * b_im + co_im[..., None] * b_re
    steps = jnp.arange(1, SUBLANES + 1, dtype=F32)[:, None, None]
    pmag = jnp.exp(steps * (a_re * dt)[None])
    pw_re = (pmag * jnp.cos(steps * (a_im * dt)[None])).reshape(SUBLANES, -1)
    pw_im = (pmag * jnp.sin(steps * (a_im * dt)[None])).reshape(SUBLANES, -1)
    gpb = S5_GROUPS // S5_BLOCKS
    eye = jnp.eye(gpb, dtype=F32)

    def blockdiag_in(bb):
        bbk = bb.reshape(S5_BLOCKS, gpb, S5_STATE, S5_CH)
        return jnp.einsum('kgpc,gh->kgchp', bbk, eye).reshape(S5_BLOCKS, gpb * S5_CH, gpb * S5_STATE)

    def blockdiag_out(cc):
        cck = cc.reshape(S5_BLOCKS, gpb, S5_CH, S5_STATE)
        return jnp.einsum('kgcp,gh->kgphc', cck, eye).reshape(S5_BLOCKS, gpb * S5_STATE, gpb * S5_CH)

    bd = jnp.concatenate([blockdiag_in(bb_re), blockdiag_in(bb_im)], axis=-1).astype(BF16)
    cd = jnp.concatenate([blockdiag_out(c_re), blockdiag_out(-c_im)], axis=1).astype(BF16)
    return bd, cd, pw_re, pw_im, ab_re, ab_im


def _layer(x, batch, seq, att_cache, ml_state, conv_state, s5_state, p, tiles):
    tm_in, tn_in, tm_out, L, t_conv, t_s5 = tiles
    u, gates = _inproj(x, p["g"], p["w_main"], p["w_gate"], tm_in, tn_in)
    nc = seq // L
    gates_t = gates[:, :SUBLANES].reshape(batch, nc, L, SUBLANES).transpose(0, 1, 3, 2)

    if att_cache is None:
        ya = _attn_prompt(u, p["bias"], batch, seq)
        rows = min(BAND * CHUNK, seq)
        u3 = u.reshape(batch, seq, N_MAIN)
        new_k = u3[:, seq - rows:, COL_AK * D_GROUP:(COL_AK + 1) * D_GROUP]
        new_v = u3[:, seq - rows:, COL_AV * D_GROUP:(COL_AV + 1) * D_GROUP]
    else:
        ya, new_k, new_v = _attn_sample(u, att_cache[0], att_cache[1], p["bias"], batch, seq)
    new_k = new_k.reshape(batch, -1, ATT_HEADS, ATT_DH)
    new_v = new_v.reshape(batch, -1, ATT_HEADS, ATT_DH)

    c0, n0, m0 = ml_state
    m0 = jnp.broadcast_to(jnp.pad(m0, ((0, 0), (0, SUBLANES - ML_HEADS)))[:, :, None],
                          (batch, SUBLANES, ML_DH))
    ym, c_new, n_new, m_new = _mlstm(u, gates, gates_t, p["b_col"], p["b_row"], c0, n0, m0, batch, seq, L)
    m_new = m_new[:, :ML_HEADS, 0]

    yc, conv_new = _conv(u, p["conv_w"], conv_state, batch, seq, t_conv)

    ys, s5_re, s5_im = _s5(u, p["bd"], p["cd"], p["pw_re"], p["pw_im"], p["dskip"], p["wglu"], p["bglu"],
                           s5_state[0].reshape(batch, 1, -1), s5_state[1].reshape(batch, 1, -1),
                           batch, seq, t_s5)
    s5_re = s5_re.reshape(batch, S5_GROUPS, S5_STATE)
    s5_im = s5_im.reshape(batch, S5_GROUPS, S5_STATE)

    x = _outproj(ya, ym, yc, ys, x, p["w_out"], tm_out)
    return x, (new_k, new_v, c_new, n_new, m_new, conv_new, s5_re, s5_im)


def kernel(x_prompt, x_sample, cache_attn_k, cache_attn_v, state_mlstm_C, state_mlstm_n, state_mlstm_m, state_conv, state_s5_re, state_s5_im, norm_g, w_in, w_out, attn_rel_bias, mlstm_b_if, conv_w, s5_A_re, s5_A_im, s5_log_dt, s5_B_re, s5_B_im, s5_C_re, s5_C_im, s5_D, s5_w_glu, s5_b_glu, final_norm_g):
    depth = w_in.shape[0]
    bp, sp, _ = x_prompt.shape
    bs, ss, _ = x_sample.shape
    w_rows = cache_attn_k.shape[2]
    gate_lo = 9 * D_GROUP
    gate_hi = gate_lo + 2 * ML_HEADS

    xp = x_prompt.reshape(bp * sp, D_MODEL)
    xs = x_sample.reshape(bs * ss, D_MODEL)
    ml0 = (jnp.zeros((bp, ML_HEADS, ML_DH, ML_DH), F32), jnp.zeros((bp, ML_HEADS, ML_DH), F32),
           jnp.zeros((bp, ML_HEADS), F32))
    conv0 = jnp.zeros((bp, CONV_W - 1, D_GROUP), F32)
    s50 = (jnp.zeros((bp, S5_GROUPS, S5_STATE), F32), jnp.zeros((bp, S5_GROUPS, S5_STATE), F32))

    tiles_p = (1024, 768, 512, CHUNK, 512, 256)
    l_s = CHUNK if ss % CHUNK == 0 else ss
    tiles_s = (bs * ss, 768, bs * ss, l_s, ss, ss)

    st_p, st_s = [], []
    for l in range(depth):
        bd, cd, pw_re, pw_im, _, _ = _s5_params(s5_A_re[l], s5_A_im[l], s5_log_dt[l], s5_B_re[l], s5_B_im[l],
                                                s5_C_re[l], s5_C_im[l])
        b_if = mlstm_b_if[l]
        p = {
            "g": norm_g[l].reshape(1, D_MODEL),
            "w_main": jnp.concatenate([w_in[l, :, :gate_lo], w_in[l, :, gate_hi:]], axis=1).astype(BF16),
            "w_gate": jnp.pad(w_in[l, :, gate_lo:gate_hi], ((0, 0), (0, GATE_PAD - 2 * ML_HEADS))).astype(BF16),
            "w_out": w_out[l].astype(BF16),
            "b_col": jnp.pad(b_if, (0, GATE_PAD - 2 * ML_HEADS)).reshape(1, GATE_PAD),
            "b_row": b_if.reshape(SUBLANES, 1),
            "conv_w": conv_w[l],
            "bd": bd, "cd": cd, "pw_re": pw_re, "pw_im": pw_im,
            "dskip": s5_D[l].reshape(1, D_GROUP),
            "wglu": s5_w_glu[l].astype(BF16),
            "bglu": s5_b_glu[l].reshape(1, D_GROUP),
        }
        p_prompt = dict(p, bias=_rel_bias_table(attn_rel_bias[l], CHUNK, (BAND + 1) * CHUNK, BAND * CHUNK))
        p_sample = dict(p, bias=_rel_bias_table(attn_rel_bias[l], ss, w_rows + ss, w_rows))

        xp, sp_l = _layer(xp, bp, sp, None, ml0, conv0, s50, p_prompt, tiles_p)
        xs, ss_l = _layer(xs, bs, ss,
                          (cache_attn_k[l].reshape(bs, w_rows, D_GROUP), cache_attn_v[l].reshape(bs, w_rows, D_GROUP)),
                          (state_mlstm_C[l], state_mlstm_n[l], state_mlstm_m[l]),
                          state_conv[l], (state_s5_re[l], state_s5_im[l]), p_sample, tiles_s)
        st_p.append(sp_l)
        st_s.append(ss_l)

    g_fin = final_norm_g.reshape(1, D_MODEL)
    y_prompt = _final_norm(xp, g_fin, 512).reshape(bp, sp, D_MODEL)
    y_sample = _final_norm(xs, g_fin, bs * ss).reshape(bs, ss, D_MODEL)
    outs_p = [jnp.stack(t) for t in zip(*st_p)]
    outs_s = [jnp.stack(t) for t in zip(*st_s)]
    return (y_prompt, y_sample, *outs_p, *outs_s)
```

```python
import functools
import math

import jax
import jax.numpy as jnp
from jax import lax
from jax.experimental import pallas as pl
from jax.experimental.pallas import tpu as pltpu

F32 = jnp.float32
BF16 = jnp.bfloat16

D_MODEL = 2048
D_GROUP = 512
CHUNK = 64
BAND = 8
REL_CLIP = 128
ATT_HEADS = 8
ATT_DH = 64
ML_HEADS = 4
ML_DH = 128
CONV_W = 3
S5_GROUPS = 32
S5_CH = 16
S5_STATE = 64
S5_BLOCKS = 4
S5_BLK_STATES = S5_GROUPS * S5_STATE // S5_BLOCKS
NORM_EPS = 1e-6
N_MAIN = 15 * D_GROUP
GATE_PAD = 128
SUBLANES = 8
LANES = 128
VMEM_LIMIT = 56 * 1024 * 1024

COL_AQ, COL_AK, COL_AV, COL_AZ = 0, 1, 2, 3
COL_MQ, COL_MK, COL_MV, COL_MO, COL_MZ = 4, 5, 6, 7, 8
COL_CB, COL_CC, COL_CX, COL_CZ = 9, 10, 11, 12
COL_SU, COL_SZ = 13, 14


def _params(n_axes):
    return pltpu.CompilerParams(dimension_semantics=("arbitrary",) * n_axes,
                                vmem_limit_bytes=VMEM_LIMIT)


def _sigmoid(x):
    return 1.0 / (1.0 + jnp.exp(-x))


def _silu(x):
    return x * _sigmoid(x)


def _log_sigmoid(x):
    return jnp.minimum(x, 0.0) - jnp.log(1.0 + jnp.exp(-jnp.abs(x)))


def _gelu_tanh(x):
    return 0.5 * x * (1.0 + jnp.tanh(math.sqrt(2.0 / math.pi) * (x + 0.044715 * (x * x * x))))


def _rms(x, g):
    return x * lax.rsqrt(jnp.mean(x * x, axis=-1, keepdims=True) + NORM_EPS) * g


def _dot(a, b):
    return jnp.dot(a, b, preferred_element_type=F32)


def _dot_nt(a, b):
    return lax.dot_general(a, b, (((1,), (1,)), ((), ())), preferred_element_type=F32)


def _dot_tn(a, b):
    return lax.dot_general(a, b, (((0,), (0,)), ((), ())), preferred_element_type=F32)


def _dot_exact(a, b):
    return jnp.dot(a, b, preferred_element_type=F32, precision=lax.Precision.HIGHEST)


def _inproj_kernel(x_ref, g_ref, w_ref, wg_ref, u_ref, gate_ref, xn_ref):
    @pl.when(pl.program_id(1) == 0)
    def _():
        xn_ref[...] = _rms(x_ref[...], g_ref[...]).astype(BF16)
        gate_ref[...] = _dot(xn_ref[...], wg_ref[...])

    u_ref[...] = _dot(xn_ref[...], w_ref[...]).astype(u_ref.dtype)


def _inproj(x, g, w_main, w_gate, tm, tn):
    n = x.shape[0]
    return pl.pallas_call(
        _inproj_kernel,
        grid=(n // tm, N_MAIN // tn),
        in_specs=[
            pl.BlockSpec((tm, D_MODEL), lambda i, j: (i, 0)),
            pl.BlockSpec((1, D_MODEL), lambda i, j: (0, 0)),
            pl.BlockSpec((D_MODEL, tn), lambda i, j: (0, j)),
            pl.BlockSpec((D_MODEL, GATE_PAD), lambda i, j: (0, 0)),
        ],
        out_specs=[
            pl.BlockSpec((tm, tn), lambda i, j: (i, j)),
            pl.BlockSpec((tm, GATE_PAD), lambda i, j: (i, 0)),
        ],
        out_shape=[
            jax.ShapeDtypeStruct((n, N_MAIN), F32),
            jax.ShapeDtypeStruct((n, GATE_PAD), F32),
        ],
        scratch_shapes=[pltpu.VMEM((tm, D_MODEL), BF16)],
        compiler_params=_params(2),
        name="inproj",
    )(x, g, w_main, w_gate)


def _outproj_kernel(ya_ref, ym_ref, yc_ref, ys_ref, x_ref, w_ref, o_ref):
    acc = x_ref[...]
    for k, y_ref in enumerate((ya_ref, ym_ref, yc_ref, ys_ref)):
        acc = acc + _dot(y_ref[...], w_ref[k * D_GROUP:(k + 1) * D_GROUP, :])
    o_ref[...] = acc


def _outproj(ya, ym, yc, ys, x, w_out, tm):
    n = x.shape[0]
    yspec = pl.BlockSpec((tm, D_GROUP), lambda i: (i, 0))
    return pl.pallas_call(
        _outproj_kernel,
        grid=(n // tm,),
        in_specs=[yspec, yspec, yspec, yspec,
                  pl.BlockSpec((tm, D_MODEL), lambda i: (i, 0)),
                  pl.BlockSpec((D_MODEL, D_MODEL), lambda i: (0, 0))],
        out_specs=pl.BlockSpec((tm, D_MODEL), lambda i: (i, 0)),
        out_shape=jax.ShapeDtypeStruct((n, D_MODEL), F32),
        compiler_params=_params(1),
        name="outproj",
    )(ya, ym, yc, ys, x, w_out)


def _final_norm_kernel(x_ref, g_ref, o_ref):
    o_ref[...] = _rms(x_ref[...], g_ref[...])


def _final_norm(x, g, tm):
    n = x.shape[0]
    return pl.pallas_call(
        _final_norm_kernel,
        grid=(n // tm,),
        in_specs=[pl.BlockSpec((tm, D_MODEL), lambda i: (i, 0)),
                  pl.BlockSpec((1, D_MODEL), lambda i: (0, 0))],
        out_specs=pl.BlockSpec((tm, D_MODEL), lambda i: (i, 0)),
        out_shape=jax.ShapeDtypeStruct((n, D_MODEL), F32),
        compiler_params=_params(1),
        name="final_norm",
    )(x, g)


def _attn_heads(q, z, k_all, v_all, bias_ref, valid, o_ref):
    pair_w = 2 * ATT_DH
    first = lax.broadcasted_iota(jnp.int32, (q.shape[0], pair_w), 1) < ATT_DH
    for pair in range(ATT_HEADS // 2):
        sl = slice(pair * pair_w, (pair + 1) * pair_w)
        qp = q[:, sl] * (ATT_DH ** -0.5)
        kp, vp = k_all[:, sl], v_all[:, sl]
        outs = []
        for sub in range(2):
            qh = jnp.where(first if sub == 0 else jnp.logical_not(first), qp, 0.0).astype(BF16)
            s = _dot_nt(qh, kp) + bias_ref[2 * pair + sub]
            if valid is not None:
                s = jnp.where(valid, s, -jnp.inf)
            p = jnp.exp(s - jnp.max(s, axis=-1, keepdims=True))
            outs.append(_dot(p.astype(BF16), vp) * (1.0 / jnp.sum(p, axis=-1, keepdims=True)))
        o = jnp.where(first, outs[0], outs[1])
        o_ref[:, sl] = (_silu(z[:, sl]) * o).astype(o_ref.dtype)


def _attn_prompt_kernel(q_ref, k_ref, v_ref, z_ref, bias_ref, o_ref, kpad_ref, vpad_ref):
    step = pl.program_id(1)
    pad = BAND * CHUNK
    rows = q_ref.shape[0]
    nk = pad + rows

    @pl.when(step == 0)
    def _():
        kpad_ref[0:pad, :] = jnp.zeros((pad, D_GROUP), BF16)
        vpad_ref[0:pad, :] = jnp.zeros((pad, D_GROUP), BF16)
        kpad_ref[pad:, :] = k_ref[...].astype(BF16)
        vpad_ref[pad:, :] = v_ref[...].astype(BF16)

    start = pl.multiple_of(step * rows, rows)
    k_all = kpad_ref[pl.ds(start, nk), :]
    v_all = vpad_ref[pl.ds(start, nk), :]
    kcol = lax.broadcasted_iota(jnp.int32, (rows, nk), 1)
    valid = kcol >= pad - step * rows
    _attn_heads(q_ref[...], z_ref[...], k_all, v_all, bias_ref, valid, o_ref)


def _attn_prompt(u, bias, batch, seq):
    rows, nk = bias.shape[1], bias.shape[2]
    ns = seq // rows
    return pl.pallas_call(
        _attn_prompt_kernel,
        grid=(batch, ns),
        in_specs=[
            pl.BlockSpec((rows, D_GROUP), lambda b, s: (b * ns + s, COL_AQ)),
            pl.BlockSpec((seq, D_GROUP), lambda b, s: (b, COL_AK)),
            pl.BlockSpec((seq, D_GROUP), lambda b, s: (b, COL_AV)),
            pl.BlockSpec((rows, D_GROUP), lambda b, s: (b * ns + s, COL_AZ)),
            pl.BlockSpec((ATT_HEADS, rows, nk), lambda b, s: (0, 0, 0)),
        ],
        out_specs=pl.BlockSpec((rows, D_GROUP), lambda b, s: (b * ns + s, 0)),
        out_shape=jax.ShapeDtypeStruct((batch * seq, D_GROUP), BF16),
        scratch_shapes=[pltpu.VMEM((seq + BAND * CHUNK, D_GROUP), BF16),
                        pltpu.VMEM((seq + BAND * CHUNK, D_GROUP), BF16)],
        compiler_params=_params(2),
        name="attn_prompt",
    )(u, u, u, u, bias)


def _attn_sample_kernel(q_ref, k_ref, v_ref, z_ref, kc_ref, vc_ref, bias_ref,
                        o_ref, kn_ref, vn_ref, kall_ref, vall_ref):
    w = kc_ref.shape[1]
    t = q_ref.shape[0]
    kall_ref[0:w, :] = kc_ref[0].astype(BF16)
    vall_ref[0:w, :] = vc_ref[0].astype(BF16)
    kall_ref[w:, :] = k_ref[...].astype(BF16)
    vall_ref[w:, :] = v_ref[...].astype(BF16)
    kn_ref[0, 0:w - t, :] = kc_ref[0, t:, :]
    vn_ref[0, 0:w - t, :] = vc_ref[0, t:, :]
    kn_ref[0, w - t:, :] = k_ref[...]
    vn_ref[0, w - t:, :] = v_ref[...]
    _attn_heads(q_ref[...], z_ref[...], kall_ref[...], vall_ref[...], bias_ref, None, o_ref)


def _attn_sample(u, k_cache, v_cache, bias, batch, seq):
    w = k_cache.shape[1]
    tok = pl.BlockSpec((seq, D_GROUP), lambda b: (b, 0))
    cache = pl.BlockSpec((1, w, D_GROUP), lambda b: (b, 0, 0))
    return pl.pallas_call(
        _attn_sample_kernel,
        grid=(batch,),
        in_specs=[
            pl.BlockSpec((seq, D_GROUP), lambda b: (b, COL_AQ)),
            pl.BlockSpec((seq, D_GROUP), lambda b: (b, COL_AK)),
            pl.BlockSpec((seq, D_GROUP), lambda b: (b, COL_AV)),
            pl.BlockSpec((seq, D_GROUP), lambda b: (b, COL_AZ)),
            cache, cache,
            pl.BlockSpec((ATT_HEADS, seq, w + seq), lambda b: (0, 0, 0)),
        ],
        out_specs=[tok, cache, cache],
        out_shape=[jax.ShapeDtypeStruct((batch * seq, D_GROUP), BF16),
                   jax.ShapeDtypeStruct(k_cache.shape, F32),
                   jax.ShapeDtypeStruct(v_cache.shape, F32)],
        scratch_shapes=[pltpu.VMEM((w + seq, D_GROUP), BF16),
                        pltpu.VMEM((w + seq, D_GROUP), BF16)],
        compiler_params=_params(1),
        name="attn_sample",
    )(u, u, u, u, k_cache, v_cache, bias)


def _mlstm_kernel(q_ref, k_ref, v_ref, o_ref, z_ref, gc_ref, gr_ref, bc_ref, br_ref,
                  c0_ref, n0_ref, m0_ref,
                  y_ref, cout_ref, nout_ref, mout_ref, c_s, n_s, m_s):
    c = pl.program_id(1)
    L = q_ref.shape[0]

    @pl.when(c == 0)
    def _():
        c_s[...] = c0_ref[0]
        n_s[...] = n0_ref[0]
        m_s[...] = m0_ref[0]

    row = lax.broadcasted_iota(jnp.int32, (L, L), 0)
    col = lax.broadcasted_iota(jnp.int32, (L, L), 1)
    tril = (row >= col).astype(F32)
    triu = (row <= col).astype(F32)
    causal = row >= col

    g_col = gc_ref[...] + bc_ref[...]
    g_row = gr_ref[0, 0] + br_ref[...]
    b_col = _dot_exact(tril, _log_sigmoid(g_col))
    b_row = _dot_exact(_log_sigmoid(g_row), triu)

    m_all, n_all = m_s[...], n_s[...]
    c_all = [c_s[h] for h in range(ML_HEADS)]
    c_new, n_new, m_new_rows = [], [], []

    for h in range(ML_HEADS):
        sl = slice(h * ML_DH, (h + 1) * ML_DH)
        bt = b_col[:, ML_HEADS + h:ML_HEADS + h + 1]
        ig_t = g_col[:, h:h + 1]
        bs = b_row[ML_HEADS + h:ML_HEADS + h + 1, :]
        ig_s = g_row[h:h + 1, :]
        m_prev = m_all[h:h + 1, 0:1]
        n_prev = n_all[h:h + 1, :]
        c_prev = c_all[h]

        qh = q_ref[:, sl]
        kh = k_ref[:, sl] * (ML_DH ** -0.5)
        vh = v_ref[:, sl]
        qb, kb = qh.astype(BF16), kh.astype(BF16)

        d = jnp.where(causal, bt - bs + ig_s, -jnp.inf)
        m_inter = bt + m_prev
        m_t = jnp.maximum(m_inter, jnp.max(d, axis=-1, keepdims=True))
        w_intra = jnp.exp(d - m_t) * _dot_nt(qb, kb)
        w_inter = jnp.exp(m_inter - m_t)
        num = w_inter * _dot_nt(qb, c_prev.astype(BF16)) + _dot(w_intra.astype(BF16), vh.astype(BF16))
        den = (w_inter * jnp.sum(qh * n_prev, axis=-1, keepdims=True)
               + jnp.sum(w_intra, axis=-1, keepdims=True))
        hh = num / jnp.maximum(jnp.abs(den), jnp.exp(-m_t))
        ym = _sigmoid(o_ref[:, sl]) * hh
        y_ref[:, sl] = (_silu(z_ref[:, sl]) * ym).astype(y_ref.dtype)

        b_last = bt[L - 1:L, :]
        m_new = jnp.maximum(b_last + m_prev, jnp.max(b_last - bs + ig_s, axis=-1, keepdims=True))
        ws = jnp.exp(b_last - bt + ig_t - m_new)
        decay = jnp.exp(b_last + m_prev - m_new)
        c_new.append(decay * c_prev + _dot_tn((vh * ws).astype(BF16), kb))
        n_new.append(decay * n_prev + jnp.sum(ws * kh, axis=0, keepdims=True))
        m_new_rows.append(jnp.broadcast_to(m_new, (1, m_s.shape[1])))

    for h in range(ML_HEADS):
        c_s[h] = c_new[h]
        cout_ref[0, h] = c_new[h]
    n_cat = jnp.concatenate(n_new, axis=0)
    n_s[...] = n_cat
    nout_ref[0] = n_cat
    m_cat = jnp.concatenate(m_new_rows + [m_all[ML_HEADS:, :]], axis=0)
    m_s[...] = m_cat
    mout_ref[0] = m_cat


def _mlstm(u, gates, gates_t, b_col, b_row, c0, n0, m0, batch, seq, L):
    nc = seq // L

    def tok(colblk):
        return pl.BlockSpec((L, D_GROUP), lambda b, c: (b * nc + c, colblk))

    st_c = pl.BlockSpec((1, ML_HEADS, ML_DH, ML_DH), lambda b, c: (b, 0, 0, 0))
    st_n = pl.BlockSpec((1, ML_HEADS, ML_DH), lambda b, c: (b, 0, 0))
    st_m = pl.BlockSpec((1, SUBLANES, ML_DH), lambda b, c: (b, 0, 0))
    return pl.pallas_call(
        _mlstm_kernel,
        grid=(batch, nc),
        in_specs=[
            tok(COL_MQ), tok(COL_MK), tok(COL_MV), tok(COL_MO), tok(COL_MZ),
            pl.BlockSpec((L, GATE_PAD), lambda b, c: (b * nc + c, 0)),
            pl.BlockSpec((1, 1, SUBLANES, L), lambda b, c: (b, c, 0, 0)),
            pl.BlockSpec((1, GATE_PAD), lambda b, c: (0, 0)),
            pl.BlockSpec((SUBLANES, 1), lambda b, c: (0, 0)),
            st_c, st_n, st_m,
        ],
        out_specs=[pl.BlockSpec((L, D_GROUP), lambda b, c: (b * nc + c, 0)), st_c, st_n, st_m],
        out_shape=[jax.ShapeDtypeStruct((batch * seq, D_GROUP), BF16),
                   jax.ShapeDtypeStruct((batch, ML_HEADS, ML_DH, ML_DH), F32),
                   jax.ShapeDtypeStruct((batch, ML_HEADS, ML_DH), F32),
                   jax.ShapeDtypeStruct((batch, SUBLANES, ML_DH), F32)],
        scratch_shapes=[pltpu.VMEM((ML_HEADS, ML_DH, ML_DH), F32),
                        pltpu.VMEM((ML_HEADS, ML_DH), F32),
                        pltpu.VMEM((SUBLANES, ML_DH), F32)],
        compiler_params=_params(2),
        name="mlstm",
    )(u, u, u, u, u, gates, gates_t, b_col, b_row, c0, n0, m0)


def _conv_kernel(bg_ref, cg_ref, x_ref, z_ref, w_ref, s0_ref, y_ref, sout_ref, u_s):
    t = pl.program_id(1)
    T = x_ref.shape[0]
    head = SUBLANES

    @pl.when(t == 0)
    def _():
        u_s[head - (CONV_W - 1):head, :] = s0_ref[0]

    u_s[head:, :] = cg_ref[...] * x_ref[...]
    y = w_ref[0:1, :] * u_s[head - 2:head - 2 + T, :]
    y = y + w_ref[1:2, :] * u_s[head - 1:head - 1 + T, :]
    y = y + w_ref[2:3, :] * u_s[head:, :]
    y_ref[...] = (_silu(z_ref[...]) * (bg_ref[...] * y)).astype(y_ref.dtype)
    last = u_s[head + T - (CONV_W - 1):, :]
    sout_ref[0] = last
    u_s[head - (CONV_W - 1):head, :] = last


def _conv(u, w, s0, batch, seq, T):
    nt = seq // T

    def tok(colblk):
        return pl.BlockSpec((T, D_GROUP), lambda b, t: (b * nt + t, colblk))

    st = pl.BlockSpec((1, CONV_W - 1, D_GROUP), lambda b, t: (b, 0, 0))
    return pl.pallas_call(
        _conv_kernel,
        grid=(batch, nt),
        in_specs=[tok(COL_CB), tok(COL_CC), tok(COL_CX), tok(COL_CZ),
                  pl.BlockSpec((CONV_W, D_GROUP), lambda b, t: (0, 0)), st],
        out_specs=[pl.BlockSpec((T, D_GROUP), lambda b, t: (b * nt + t, 0)), st],
        out_shape=[jax.ShapeDtypeStruct((batch * seq, D_GROUP), BF16),
                   jax.ShapeDtypeStruct((batch, CONV_W - 1, D_GROUP), F32)],
        scratch_shapes=[pltpu.VMEM((SUBLANES + T, D_GROUP), F32)],
        compiler_params=_params(2),
        name="conv",
    )(u, u, u, u, w, s0)


def _s5_kernel(u_ref, z_ref, bd_ref, cd_ref, are_ref, aim_ref, dskip_ref, wglu_ref, bglu_ref,
               s0re_ref, s0im_ref, y_ref, sre_ref, sim_ref, perm_s, up_s, xr_s, xi_s, car_s, cai_s):
    t = pl.program_id(1)
    nseq, T = u_ref.shape[0], u_ref.shape[1]
    rows = nseq * T
    NS = S5_BLK_STATES
    ntile = D_GROUP // LANES

    @pl.when(t == 0)
    def _():
        car_s[...] = s0re_ref[...]
        cai_s[...] = s0im_ref[...]

    u_seq = u_ref[...].reshape(rows, D_GROUP)
    for j in range(ntile):
        perm_s[j] = u_seq[:, j * LANES:(j + 1) * LANES]

    def to_frame_major(g, carry):
        dst = pl.ds(pl.multiple_of(g * nseq, nseq), nseq)
        for j in range(ntile):
            up_s[dst, j * LANES:(j + 1) * LANES] = perm_s[j, pl.ds(g, nseq, stride=T), :]
        return carry

    lax.fori_loop(0, T, to_frame_major, 0, unroll=8)
    u = up_s[...]
    ub = u.astype(BF16)
    for blk in range(S5_BLOCKS):
        cols = slice(blk * NS, (blk + 1) * NS)
        bu = _dot(ub[:, blk * 128:(blk + 1) * 128], bd_ref[blk])
        xr_s[:, cols] = bu[:, :NS]
        xi_s[:, cols] = bu[:, NS:]

    for blk in range(S5_BLOCKS):
        cols = slice(blk * NS, (blk + 1) * NS)
        ar = are_ref[:, cols]
        ai = aim_ref[:, cols]

        def body(g, carry, cols=cols, ar=ar, ai=ai):
            xr, xi = carry
            frame = pl.ds(pl.multiple_of(g * nseq, nseq), nseq)
            nr = ar * xr - ai * xi + xr_s[frame, cols]
            ni = ar * xi + ai * xr + xi_s[frame, cols]
            xr_s[frame, cols] = nr
            xi_s[frame, cols] = ni
            return nr, ni

        cr, ci = lax.fori_loop(0, T, body, (car_s[:, cols], cai_s[:, cols]), unroll=4)
        car_s[:, cols] = cr
        cai_s[:, cols] = ci

    ys = []
    for blk in range(S5_BLOCKS):
        cols = slice(blk * NS, (blk + 1) * NS)
        ys.append(_dot(xr_s[:, cols].astype(BF16), cd_ref[blk, :NS, :])
                  + _dot(xi_s[:, cols].astype(BF16), cd_ref[blk, NS:, :]))
    y = _gelu_tanh(jnp.concatenate(ys, axis=1) + dskip_ref[...] * u)
    y = y * _sigmoid(_dot(y.astype(BF16), wglu_ref[...]) + bglu_ref[...])
    for j in range(ntile):
        perm_s[j] = y[:, j * LANES:(j + 1) * LANES]
    for i in range(nseq):
        y_i = jnp.concatenate([perm_s[j, pl.ds(i, T, stride=nseq), :] for j in range(ntile)], axis=1)
        y_ref[i] = (_silu(z_ref[i]) * y_i).astype(y_ref.dtype)
    sre_ref[...] = car_s[...]
    sim_ref[...] = cai_s[...]


def _s5(u, bd, cd, a_re, a_im, dskip, wglu, bglu, s0_re, s0_im, batch, seq, T):
    nseq = SUBLANES
    nt = seq // T
    ns_all = S5_GROUPS * S5_STATE
    u3 = u.reshape(batch, seq, N_MAIN)

    def tok(colblk):
        return pl.BlockSpec((nseq, T, D_GROUP), lambda b, t: (b, t, colblk))

    def full(shape):
        return pl.BlockSpec(shape, lambda b, t: (0,) * len(shape))

    st = pl.BlockSpec((nseq, ns_all), lambda b, t: (b, 0))
    y, s_re, s_im = pl.pallas_call(
        _s5_kernel,
        grid=(batch // nseq, nt),
        in_specs=[tok(COL_SU), tok(COL_SZ),
                  full(bd.shape), full(cd.shape), full(a_re.shape), full(a_im.shape),
                  full(dskip.shape), full(wglu.shape), full(bglu.shape), st, st],
        out_specs=[pl.BlockSpec((nseq, T, D_GROUP), lambda b, t: (b, t, 0)), st, st],
        out_shape=[jax.ShapeDtypeStruct((batch, seq, D_GROUP), BF16),
                   jax.ShapeDtypeStruct((batch, ns_all), F32),
                   jax.ShapeDtypeStruct((batch, ns_all), F32)],
        scratch_shapes=[pltpu.VMEM((D_GROUP // LANES, nseq * T, LANES), F32),
                        pltpu.VMEM((nseq * T, D_GROUP), F32),
                        pltpu.VMEM((nseq * T, ns_all), F32), pltpu.VMEM((nseq * T, ns_all), F32),
                        pltpu.VMEM((nseq, ns_all), F32), pltpu.VMEM((nseq, ns_all), F32)],
        compiler_params=_params(2),
        name="s5",
    )(u3, u3, bd, cd, a_re, a_im, dskip, wglu, bglu, s0_re, s0_im)
    return y.reshape(batch * seq, D_GROUP), s_re, s_im


def _rel_bias_table(rel_bias, n_q, n_k, offset, band):
    j = jnp.arange(n_k + n_q - 1)
    idx = jnp.clip(offset + n_q - 1 - j, -REL_CLIP, REL_CLIP) + REL_CLIP
    ext = rel_bias[..., idx].astype(F32)
    tbl = jnp.stack([ext[..., n_q - 1 - i:n_q - 1 - i + n_k] for i in range(n_q)], axis=-2)
    if band:
        qc = jnp.arange(n_q)[:, None] // CHUNK
        kc = jnp.arange(n_k)[None, :] // CHUNK
        tbl = jnp.where((kc >= qc) & (kc <= qc + BAND), tbl, -jnp.inf)
    return tbl


def _s5_params(a_re, a_im, log_dt, b_re, b_im, c_re, c_im):
    dt = jnp.exp(log_dt)[:, None]
    mag = jnp.exp(a_re * dt)
    ab_re, ab_im = mag * jnp.cos(a_im * dt), mag * jnp.sin(a_im * dt)
    den = a_re * a_re + a_im * a_im
    co_re = ((ab_re - 1.0) * a_re + ab_im * a_im) / den
    co_im = (ab_im * a_re - (ab_re - 1.0) * a_im) / den
    bb_re = co_re[..., None] * b_re - co_im[..., None] * b_im
    bb_im = co_re[..., None] * b_im + co_im[..., None] * b_re
    a8_re = jnp.broadcast_to(ab_re.reshape(1, -1), (SUBLANES, S5_GROUPS * S5_STATE))
    a8_im = jnp.broadcast_to(ab_im.reshape(1, -1), (SUBLANES, S5_GROUPS * S5_STATE))
    gpb = S5_GROUPS // S5_BLOCKS
    eye = jnp.eye(gpb, dtype=F32)

    def blockdiag_in(bb):
        bbk = bb.reshape(S5_BLOCKS, gpb, S5_STATE, S5_CH)
        return jnp.einsum('kgpc,gh->kgchp', bbk, eye).reshape(S5_BLOCKS, gpb * S5_CH, gpb * S5_STATE)

    def blockdiag_out(cc):
        cck = cc.reshape(S5_BLOCKS, gpb, S5_CH, S5_STATE)
        return jnp.einsum('kgcp,gh->kgphc', cck, eye).reshape(S5_BLOCKS, gpb * S5_STATE, gpb * S5_CH)

    bd = jnp.concatenate([blockdiag_in(bb_re), blockdiag_in(bb_im)], axis=-1).astype(BF16)
    cd = jnp.concatenate([blockdiag_out(c_re), blockdiag_out(-c_im)], axis=1).astype(BF16)
    return bd, cd, a8_re, a8_im


def _layer(x, batch, seq, att_cache, ml_state, conv_state, s5_state, p, tiles):
    tm_in, tn_in, tm_out, L, t_conv, t_s5 = tiles
    u, gates = _inproj(x, p["g"], p["w_main"], p["w_gate"], tm_in, tn_in)
    nc = seq // L
    gates_t = gates[:, :SUBLANES].reshape(batch, nc, L, SUBLANES).transpose(0, 1, 3, 2)

    if att_cache is None:
        ya = _attn_prompt(u, p["bias"], batch, seq)
        rows = min(BAND * CHUNK, seq)
        u3 = u.reshape(batch, seq, N_MAIN)
        new_k = u3[:, seq - rows:, COL_AK * D_GROUP:(COL_AK + 1) * D_GROUP]
        new_v = u3[:, seq - rows:, COL_AV * D_GROUP:(COL_AV + 1) * D_GROUP]
    else:
        ya, new_k, new_v = _attn_sample(u, att_cache[0], att_cache[1], p["bias"], batch, seq)
    new_k = new_k.reshape(batch, -1, ATT_HEADS, ATT_DH)
    new_v = new_v.reshape(batch, -1, ATT_HEADS, ATT_DH)

    c0, n0, m0 = ml_state
    m0 = jnp.broadcast_to(jnp.pad(m0, ((0, 0), (0, SUBLANES - ML_HEADS)))[:, :, None],
                          (batch, SUBLANES, ML_DH))
    ym, c_new, n_new, m_new = _mlstm(u, gates, gates_t, p["b_col"], p["b_row"], c0, n0, m0, batch, seq, L)
    m_new = m_new[:, :ML_HEADS, 0]

    yc, conv_new = _conv(u, p["conv_w"], conv_state, batch, seq, t_conv)

    ys, s5_re, s5_im = _s5(u, p["bd"], p["cd"], p["a_re"], p["a_im"], p["dskip"], p["wglu"], p["bglu"],
                           s5_state[0].reshape(batch, -1), s5_state[1].reshape(batch, -1),
                           batch, seq, t_s5)
    s5_re = s5_re.reshape(batch, S5_GROUPS, S5_STATE)
    s5_im = s5_im.reshape(batch, S5_GROUPS, S5_STATE)

    x = _outproj(ya, ym, yc, ys, x, p["w_out"], tm_out)
    return x, (new_k, new_v, c_new, n_new, m_new, conv_new, s5_re, s5_im)


def kernel(x_prompt, x_sample, cache_attn_k, cache_attn_v, state_mlstm_C, state_mlstm_n, state_mlstm_m, state_conv, state_s5_re, state_s5_im, norm_g, w_in, w_out, attn_rel_bias, mlstm_b_if, conv_w, s5_A_re, s5_A_im, s5_log_dt, s5_B_re, s5_B_im, s5_C_re, s5_C_im, s5_D, s5_w_glu, s5_b_glu, final_norm_g):
    depth = w_in.shape[0]
    bp, sp, _ = x_prompt.shape
    bs, ss, _ = x_sample.shape
    w_rows = cache_attn_k.shape[2]
    gate_lo = 9 * D_GROUP
    gate_hi = gate_lo + 2 * ML_HEADS

    xp = x_prompt.reshape(bp * sp, D_MODEL)
    xs = x_sample.reshape(bs * ss, D_MODEL)
    ml0 = (jnp.zeros((bp, ML_HEADS, ML_DH, ML_DH), F32), jnp.zeros((bp, ML_HEADS, ML_DH), F32),
           jnp.zeros((bp, ML_HEADS), F32))
    conv0 = jnp.zeros((bp, CONV_W - 1, D_GROUP), F32)
    s50 = (jnp.zeros((bp, S5_GROUPS, S5_STATE), F32), jnp.zeros((bp, S5_GROUPS, S5_STATE), F32))

    tiles_p = (1024, 768, 512, 4 * CHUNK, 512, 128)
    tiles_s = (bs * ss, 768, bs * ss, ss, ss, ss)
    att_rows = 4 * CHUNK
    bias_p = _rel_bias_table(attn_rel_bias, att_rows, BAND * CHUNK + att_rows, BAND * CHUNK, True)
    bias_s = _rel_bias_table(attn_rel_bias, ss, w_rows + ss, w_rows, False)

    st_p, st_s = [], []
    for l in range(depth):
        bd, cd, a8_re, a8_im = _s5_params(s5_A_re[l], s5_A_im[l], s5_log_dt[l], s5_B_re[l], s5_B_im[l],
                                          s5_C_re[l], s5_C_im[l])
        b_if = mlstm_b_if[l]
        p = {
            "g": norm_g[l].reshape(1, D_MODEL),
            "w_main": jnp.concatenate([w_in[l, :, :gate_lo], w_in[l, :, gate_hi:]], axis=1).astype(BF16),
            "w_gate": jnp.pad(w_in[l, :, gate_lo:gate_hi], ((0, 0), (0, GATE_PAD - 2 * ML_HEADS))).astype(BF16),
            "w_out": w_out[l].astype(BF16),
            "b_col": jnp.pad(b_if, (0, GATE_PAD - 2 * ML_HEADS)).reshape(1, GATE_PAD),
            "b_row": b_if.reshape(SUBLANES, 1),
            "conv_w": conv_w[l],
            "bd": bd, "cd": cd, "a_re": a8_re, "a_im": a8_im,
            "dskip": s5_D[l].reshape(1, D_GROUP),
            "wglu": s5_w_glu[l].astype(BF16),
            "bglu": s5_b_glu[l].reshape(1, D_GROUP),
        }
        p_prompt = dict(p, bias=bias_p[l])
        p_sample = dict(p, bias=bias_s[l])

        xp, sp_l = _layer(xp, bp, sp, None, ml0, conv0, s50, p_prompt, tiles_p)
        xs, ss_l = _layer(xs, bs, ss,
                          (cache_attn_k[l].reshape(bs, w_rows, D_GROUP), cache_attn_v[l].reshape(bs, w_rows, D_GROUP)),
                          (state_mlstm_C[l], state_mlstm_n[l], state_mlstm_m[l]),
                          state_conv[l], (state_s5_re[l], state_s5_im[l]), p_sample, tiles_s)
        st_p.append(sp_l)
        st_s.append(ss_l)

    g_fin = final_norm_g.reshape(1, D_MODEL)
    y_prompt = _final_norm(xp, g_fin, 512).reshape(bp, sp, D_MODEL)
    y_sample = _final_norm(xs, g_fin, bs * ss).reshape(bs, ss, D_MODEL)
    outs_p = [jnp.stack(t) for t in zip(*st_p)]
    outs_s = [jnp.stack(t) for t in zip(*st_s)]
    return (y_prompt, y_sample, *outs_p, *outs_s)
```

```python
import functools
import math

import jax
import jax.numpy as jnp
from jax import lax
from jax.experimental import pallas as pl
from jax.experimental.pallas import tpu as pltpu

F32 = jnp.float32
BF16 = jnp.bfloat16

D_MODEL = 2048
D_GROUP = 512
CHUNK = 64
BAND = 8
REL_CLIP = 128
ATT_HEADS = 8
ATT_DH = 64
ML_HEADS = 4
ML_DH = 128
CONV_W = 3
S5_GROUPS = 32
S5_CH = 16
S5_STATE = 64
S5_BLOCKS = 4
S5_BLK_STATES = S5_GROUPS * S5_STATE // S5_BLOCKS
NORM_EPS = 1e-6
N_MAIN = 15 * D_GROUP
GATE_PAD = 128
SUBLANES = 8
LANES = 128
VMEM_LIMIT = 56 * 1024 * 1024

COL_AQ, COL_AK, COL_AV, COL_AZ = 0, 1, 2, 3
COL_MQ, COL_MK, COL_MV, COL_MO, COL_MZ = 4, 5, 6, 7, 8
COL_CB, COL_CC, COL_CX, COL_CZ = 9, 10, 11, 12
COL_SU, COL_SZ = 13, 14


def _params(n_axes):
    return pltpu.CompilerParams(dimension_semantics=("arbitrary",) * n_axes,
                                vmem_limit_bytes=VMEM_LIMIT)


def _sigmoid(x):
    return 1.0 / (1.0 + jnp.exp(-x))


def _silu(x):
    return x * _sigmoid(x)


def _log_sigmoid(x):
    return jnp.minimum(x, 0.0) - jnp.log(1.0 + jnp.exp(-jnp.abs(x)))


def _gelu_tanh(x):
    return 0.5 * x * (1.0 + jnp.tanh(math.sqrt(2.0 / math.pi) * (x + 0.044715 * (x * x * x))))


def _rms(x, g):
    return x * lax.rsqrt(jnp.mean(x * x, axis=-1, keepdims=True) + NORM_EPS) * g


def _dot(a, b):
    return jnp.dot(a, b, preferred_element_type=F32)


def _dot_nt(a, b):
    return lax.dot_general(a, b, (((1,), (1,)), ((), ())), preferred_element_type=F32)


def _dot_tn(a, b):
    return lax.dot_general(a, b, (((0,), (0,)), ((), ())), preferred_element_type=F32)


def _dot_exact(a, b):
    return jnp.dot(a, b, preferred_element_type=F32, precision=lax.Precision.HIGHEST)


def _prenorm_kernel(x_ref, g_ref, o_ref):
    o_ref[...] = _rms(x_ref[...], g_ref[...]).astype(o_ref.dtype)


def _prenorm(x, g, tm):
    n = x.shape[0]
    return pl.pallas_call(
        _prenorm_kernel,
        grid=(n // tm,),
        in_specs=[pl.BlockSpec((tm, D_MODEL), lambda i: (i, 0)),
                  pl.BlockSpec((1, D_MODEL), lambda i: (0, 0))],
        out_specs=pl.BlockSpec((tm, D_MODEL), lambda i: (i, 0)),
        out_shape=jax.ShapeDtypeStruct((n, D_MODEL), BF16),
        compiler_params=_params(1),
        name="prenorm",
    )(x, g)


def _inproj_kernel(xn_ref, w_ref, wg_ref, u_ref, gate_ref):
    @pl.when(pl.program_id(1) == 0)
    def _():
        gate_ref[...] = _dot(xn_ref[...], wg_ref[...])

    u_ref[...] = _dot(xn_ref[...], w_ref[...]).astype(u_ref.dtype)


def _inproj(xn, w_all, layer, tm, tn):
    n = xn.shape[0]
    return pl.pallas_call(
        _inproj_kernel,
        grid=(n // tm, N_MAIN // tn),
        in_specs=[
            pl.BlockSpec((tm, D_MODEL), lambda i, j: (i, 0)),
            pl.BlockSpec((None, D_MODEL, tn), lambda i, j: (layer, 0, j)),
            pl.BlockSpec((None, D_MODEL, GATE_PAD), lambda i, j: (layer, 0, N_MAIN // GATE_PAD)),
        ],
        out_specs=[
            pl.BlockSpec((tm, tn), lambda i, j: (i, j)),
            pl.BlockSpec((tm, GATE_PAD), lambda i, j: (i, 0)),
        ],
        out_shape=[
            jax.ShapeDtypeStruct((n, N_MAIN), BF16),
            jax.ShapeDtypeStruct((n, GATE_PAD), F32),
        ],
        compiler_params=_params(2),
        name="inproj",
    )(xn, w_all, w_all)


def _outproj_kernel(ya_ref, ym_ref, yc_ref, ys_ref, x_ref, w_ref, g_ref, *out_refs):
    acc = x_ref[...]
    for k, y_ref in enumerate((ya_ref, ym_ref, yc_ref, ys_ref)):
        acc = acc + _dot(y_ref[...], w_ref[k * D_GROUP:(k + 1) * D_GROUP, :])
    normed = _rms(acc, g_ref[...])
    if len(out_refs) == 2:
        out_refs[0][...] = acc
    out_refs[-1][...] = normed.astype(out_refs[-1].dtype)


def _outproj(ya, ym, yc, ys, x, w_out_all, layer, g_next, last, tm):
    n = x.shape[0]
    yspec = pl.BlockSpec((tm, D_GROUP), lambda i: (i, 0))
    row = pl.BlockSpec((tm, D_MODEL), lambda i: (i, 0))
    if last:
        out_specs, out_shape = [row], [jax.ShapeDtypeStruct((n, D_MODEL), F32)]
    else:
        out_specs = [row, row]
        out_shape = [jax.ShapeDtypeStruct((n, D_MODEL), F32), jax.ShapeDtypeStruct((n, D_MODEL), BF16)]
    return pl.pallas_call(
        _outproj_kernel,
        grid=(n // tm,),
        in_specs=[yspec, yspec, yspec, yspec, row,
                  pl.BlockSpec((None, D_MODEL, D_MODEL), lambda i: (layer, 0, 0)),
                  pl.BlockSpec((1, D_MODEL), lambda i: (0, 0))],
        out_specs=out_specs,
        out_shape=out_shape,
        compiler_params=_params(1),
        name="outproj",
    )(ya, ym, yc, ys, x, w_out_all, g_next)


def _attn_heads(q, z, k_all, v_all, bias_ref, first_key, o_ref):
    lq, nk = q.shape[0], k_all.shape[0]
    pair_w = 2 * ATT_DH
    first = lax.broadcasted_iota(jnp.int32, (lq, pair_w), 1) < ATT_DH
    kcol = lax.broadcasted_iota(jnp.int32, (2 * lq, nk), 1)
    ones = jnp.ones((nk, pair_w), BF16)
    for pair in range(ATT_HEADS // 2):
        sl = slice(pair * pair_w, (pair + 1) * pair_w)
        qp = q[:, sl] * (ATT_DH ** -0.5)
        kp, vp = k_all[:, sl], v_all[:, sl]
        q2 = jnp.concatenate([jnp.where(first, qp, 0.0), jnp.where(first, 0.0, qp)], axis=0).astype(BF16)
        s = _dot_nt(q2, kp) + jnp.concatenate([bias_ref[2 * pair], bias_ref[2 * pair + 1]], axis=0)
        if first_key is not None:
            s = jnp.where(kcol >= first_key, s, -jnp.inf)
        p = jnp.exp(s - jnp.max(s, axis=-1, keepdims=True)).astype(BF16)
        ol = _dot(p, jnp.concatenate([vp, ones], axis=1))
        on = ol[:, :pair_w] * (1.0 / ol[:, pair_w:])
        o = jnp.where(first, on[:lq], on[lq:])
        o_ref[:, sl] = (_silu(z[:, sl]) * o).astype(o_ref.dtype)


def _attn_prompt_kernel(q_ref, k_ref, v_ref, z_ref, bias_ref, o_ref, kpad_ref, vpad_ref):
    step = pl.program_id(1)
    pad = BAND * CHUNK
    rows = q_ref.shape[0]
    nk = pad + rows

    @pl.when(step == 0)
    def _():
        kpad_ref[0:pad, :] = jnp.zeros((pad, D_GROUP), BF16)
        vpad_ref[0:pad, :] = jnp.zeros((pad, D_GROUP), BF16)
        kpad_ref[pad:, :] = k_ref[...].astype(BF16)
        vpad_ref[pad:, :] = v_ref[...].astype(BF16)

    start = pl.multiple_of(step * rows, rows)
    k_all = kpad_ref[pl.ds(start, nk), :]
    v_all = vpad_ref[pl.ds(start, nk), :]
    _attn_heads(q_ref[...].astype(F32), z_ref[...].astype(F32), k_all, v_all, bias_ref,
                pad - step * rows, o_ref)


def _attn_prompt(u, bias, batch, seq):
    rows, nk = bias.shape[1], bias.shape[2]
    ns = seq // rows
    return pl.pallas_call(
        _attn_prompt_kernel,
        grid=(batch, ns),
        in_specs=[
            pl.BlockSpec((rows, D_GROUP), lambda b, s: (b * ns + s, COL_AQ)),
            pl.BlockSpec((seq, D_GROUP), lambda b, s: (b, COL_AK)),
            pl.BlockSpec((seq, D_GROUP), lambda b, s: (b, COL_AV)),
            pl.BlockSpec((rows, D_GROUP), lambda b, s: (b * ns + s, COL_AZ)),
            pl.BlockSpec((ATT_HEADS, rows, nk), lambda b, s: (0, 0, 0)),
        ],
        out_specs=pl.BlockSpec((rows, D_GROUP), lambda b, s: (b * ns + s, 0)),
        out_shape=jax.ShapeDtypeStruct((batch * seq, D_GROUP), BF16),
        scratch_shapes=[pltpu.VMEM((seq + BAND * CHUNK, D_GROUP), BF16),
                        pltpu.VMEM((seq + BAND * CHUNK, D_GROUP), BF16)],
        compiler_params=_params(2),
        name="attn_prompt",
    )(u, u, u, u, bias)


def _attn_sample_kernel(q_ref, k_ref, v_ref, z_ref, kc_ref, vc_ref, bias_ref,
                        o_ref, kn_ref, vn_ref, kall_ref, vall_ref):
    w = kc_ref.shape[1]
    t = q_ref.shape[0]
    kall_ref[0:w, :] = kc_ref[0].astype(BF16)
    vall_ref[0:w, :] = vc_ref[0].astype(BF16)
    kall_ref[w:, :] = k_ref[...].astype(BF16)
    vall_ref[w:, :] = v_ref[...].astype(BF16)
    kn_ref[0, 0:w - t, :] = kc_ref[0, t:, :]
    vn_ref[0, 0:w - t, :] = vc_ref[0, t:, :]
    kn_ref[0, w - t:, :] = k_ref[...].astype(F32)
    vn_ref[0, w - t:, :] = v_ref[...].astype(F32)
    _attn_heads(q_ref[...].astype(F32), z_ref[...].astype(F32), kall_ref[...], vall_ref[...], bias_ref, None, o_ref)


def _attn_sample(u, k_cache, v_cache, bias, batch, seq):
    w = k_cache.shape[1]
    tok = pl.BlockSpec((seq, D_GROUP), lambda b: (b, 0))
    cache = pl.BlockSpec((1, w, D_GROUP), lambda b: (b, 0, 0))
    return pl.pallas_call(
        _attn_sample_kernel,
        grid=(batch,),
        in_specs=[
            pl.BlockSpec((seq, D_GROUP), lambda b: (b, COL_AQ)),
            pl.BlockSpec((seq, D_GROUP), lambda b: (b, COL_AK)),
            pl.BlockSpec((seq, D_GROUP), lambda b: (b, COL_AV)),
            pl.BlockSpec((seq, D_GROUP), lambda b: (b, COL_AZ)),
            cache, cache,
            pl.BlockSpec((ATT_HEADS, seq, w + seq), lambda b: (0, 0, 0)),
        ],
        out_specs=[tok, cache, cache],
        out_shape=[jax.ShapeDtypeStruct((batch * seq, D_GROUP), BF16),
                   jax.ShapeDtypeStruct(k_cache.shape, F32),
                   jax.ShapeDtypeStruct(v_cache.shape, F32)],
        scratch_shapes=[pltpu.VMEM((w + seq, D_GROUP), BF16),
                        pltpu.VMEM((w + seq, D_GROUP), BF16)],
        compiler_params=_params(1),
        name="attn_sample",
    )(u, u, u, u, k_cache, v_cache, bias)


def _mlstm_kernel(q_ref, k_ref, v_ref, o_ref, z_ref, gc_ref, gr_ref, bc_ref, br_ref,
                  c0_ref, n0_ref, m0_ref,
                  y_ref, cout_ref, nout_ref, mout_ref, c_s, n_s, m_s):
    c = pl.program_id(1)
    L = q_ref.shape[0]

    @pl.when(c == 0)
    def _():
        c_s[...] = c0_ref[0]
        n_s[...] = n0_ref[0]
        m_s[...] = m0_ref[0]

    row = lax.broadcasted_iota(jnp.int32, (L, L), 0)
    col = lax.broadcasted_iota(jnp.int32, (L, L), 1)
    tril = (row >= col).astype(F32)
    triu = (row <= col).astype(F32)
    causal = row >= col

    g_col = gc_ref[...] + bc_ref[...]
    g_row = gr_ref[0, 0] + br_ref[...]
    b_col = _dot_exact(tril, _log_sigmoid(g_col))
    b_row = _dot_exact(_log_sigmoid(g_row), triu)

    m_all, n_all = m_s[...], n_s[...]
    c_all = [c_s[h] for h in range(ML_HEADS)]
    c_new, n_new, m_new_rows = [], [], []

    for h in range(ML_HEADS):
        sl = slice(h * ML_DH, (h + 1) * ML_DH)
        bt = b_col[:, ML_HEADS + h:ML_HEADS + h + 1]
        ig_t = g_col[:, h:h + 1]
        bs = b_row[ML_HEADS + h:ML_HEADS + h + 1, :]
        ig_s = g_row[h:h + 1, :]
        m_prev = m_all[h:h + 1, 0:1]
        n_prev = n_all[h:h + 1, :]
        c_prev = c_all[h]

        qh = q_ref[:, sl].astype(F32)
        kh = k_ref[:, sl].astype(F32) * (ML_DH ** -0.5)
        vh = v_ref[:, sl].astype(F32)
        qb, kb = qh.astype(BF16), kh.astype(BF16)

        d = jnp.where(causal, bt - bs + ig_s, -jnp.inf)
        m_inter = bt + m_prev
        m_t = jnp.maximum(m_inter, jnp.max(d, axis=-1, keepdims=True))
        w_intra = jnp.exp(d - m_t) * _dot_nt(qb, kb)
        w_inter = jnp.exp(m_inter - m_t)
        num = w_inter * _dot_nt(qb, c_prev.astype(BF16)) + _dot(w_intra.astype(BF16), vh.astype(BF16))
        den = (w_inter * jnp.sum(qh * n_prev, axis=-1, keepdims=True)
               + jnp.sum(w_intra, axis=-1, keepdims=True))
        hh = num / jnp.maximum(jnp.abs(den), jnp.exp(-m_t))
        ym = _sigmoid(o_ref[:, sl].astype(F32)) * hh
        y_ref[:, sl] = (_silu(z_ref[:, sl].astype(F32)) * ym).astype(y_ref.dtype)

        b_last = bt[L - 1:L, :]
        m_new = jnp.maximum(b_last + m_prev, jnp.max(b_last - bs + ig_s, axis=-1, keepdims=True))
        ws = jnp.exp(b_last - bt + ig_t - m_new)
        decay = jnp.exp(b_last + m_prev - m_new)
        c_new.append(decay * c_prev + _dot_tn((vh * ws).astype(BF16), kb))
        n_new.append(decay * n_prev + jnp.sum(ws * kh, axis=0, keepdims=True))
        m_new_rows.append(jnp.broadcast_to(m_new, (1, m_s.shape[1])))

    for h in range(ML_HEADS):
        c_s[h] = c_new[h]
        cout_ref[0, h] = c_new[h]
    n_cat = jnp.concatenate(n_new, axis=0)
    n_s[...] = n_cat
    nout_ref[0] = n_cat
    m_cat = jnp.concatenate(m_new_rows + [m_all[ML_HEADS:, :]], axis=0)
    m_s[...] = m_cat
    mout_ref[0] = m_cat


def _mlstm(u, gates, gates_t, b_col, b_row, c0, n0, m0, batch, seq, L):
    nc = seq // L

    def tok(colblk):
        return pl.BlockSpec((L, D_GROUP), lambda b, c: (b * nc + c, colblk))

    st_c = pl.BlockSpec((1, ML_HEADS, ML_DH, ML_DH), lambda b, c: (b, 0, 0, 0))
    st_n = pl.BlockSpec((1, ML_HEADS, ML_DH), lambda b, c: (b, 0, 0))
    st_m = pl.BlockSpec((1, SUBLANES, ML_DH), lambda b, c: (b, 0, 0))
    return pl.pallas_call(
        _mlstm_kernel,
        grid=(batch, nc),
        in_specs=[
            tok(COL_MQ), tok(COL_MK), tok(COL_MV), tok(COL_MO), tok(COL_MZ),
            pl.BlockSpec((L, GATE_PAD), lambda b, c: (b * nc + c, 0)),
            pl.BlockSpec((1, 1, SUBLANES, L), lambda b, c: (b, c, 0, 0)),
            pl.BlockSpec((1, GATE_PAD), lambda b, c: (0, 0)),
            pl.BlockSpec((SUBLANES, 1), lambda b, c: (0, 0)),
            st_c, st_n, st_m,
        ],
        out_specs=[pl.BlockSpec((L, D_GROUP), lambda b, c: (b * nc + c, 0)), st_c, st_n, st_m],
        out_shape=[jax.ShapeDtypeStruct((batch * seq, D_GROUP), BF16),
                   jax.ShapeDtypeStruct((batch, ML_HEADS, ML_DH, ML_DH), F32),
                   jax.ShapeDtypeStruct((batch, ML_HEADS, ML_DH), F32),
                   jax.ShapeDtypeStruct((batch, SUBLANES, ML_DH), F32)],
        scratch_shapes=[pltpu.VMEM((ML_HEADS, ML_DH, ML_DH), F32),
                        pltpu.VMEM((ML_HEADS, ML_DH), F32),
                        pltpu.VMEM((SUBLANES, ML_DH), F32)],
        compiler_params=_params(2),
        name="mlstm",
    )(u, u, u, u, u, gates, gates_t, b_col, b_row, c0, n0, m0)


def _conv_kernel(bg_ref, cg_ref, x_ref, z_ref, w_ref, s0_ref, y_ref, sout_ref, u_s):
    t = pl.program_id(1)
    T = x_ref.shape[0]
    head = SUBLANES

    @pl.when(t == 0)
    def _():
        u_s[head - (CONV_W - 1):head, :] = s0_ref[0]

    u_s[head:, :] = cg_ref[...].astype(F32) * x_ref[...].astype(F32)
    y = w_ref[0:1, :] * u_s[head - 2:head - 2 + T, :]
    y = y + w_ref[1:2, :] * u_s[head - 1:head - 1 + T, :]
    y = y + w_ref[2:3, :] * u_s[head:, :]
    y_ref[...] = (_silu(z_ref[...].astype(F32)) * (bg_ref[...].astype(F32) * y)).astype(y_ref.dtype)
    last = u_s[head + T - (CONV_W - 1):, :]
    sout_ref[0] = last
    u_s[head - (CONV_W - 1):head, :] = last


def _conv(u, w, s0, batch, seq, T):
    nt = seq // T

    def tok(colblk):
        return pl.BlockSpec((T, D_GROUP), lambda b, t: (b * nt + t, colblk))

    st = pl.BlockSpec((1, CONV_W - 1, D_GROUP), lambda b, t: (b, 0, 0))
    return pl.pallas_call(
        _conv_kernel,
        grid=(batch, nt),
        in_specs=[tok(COL_CB), tok(COL_CC), tok(COL_CX), tok(COL_CZ),
                  pl.BlockSpec((CONV_W, D_GROUP), lambda b, t: (0, 0)), st],
        out_specs=[pl.BlockSpec((T, D_GROUP), lambda b, t: (b * nt + t, 0)), st],
        out_shape=[jax.ShapeDtypeStruct((batch * seq, D_GROUP), BF16),
                   jax.ShapeDtypeStruct((batch, CONV_W - 1, D_GROUP), F32)],
        scratch_shapes=[pltpu.VMEM((SUBLANES + T, D_GROUP), F32)],
        compiler_params=_params(2),
        name="conv",
    )(u, u, u, u, w, s0)


def _s5_kernel(u_ref, z_ref, bd_ref, cd_ref, are_ref, aim_ref, dskip_ref, wglu_ref, bglu_ref,
               s0re_ref, s0im_ref, y_ref, sre_ref, sim_ref, perm_s, up_s, xr_s, xi_s, car_s, cai_s):
    t = pl.program_id(1)
    nseq, T = u_ref.shape[0], u_ref.shape[1]
    rows = nseq * T
    NS = S5_BLK_STATES
    ntile = D_GROUP // LANES

    @pl.when(t == 0)
    def _():
        car_s[...] = s0re_ref[...]
        cai_s[...] = s0im_ref[...]

    u_seq = u_ref[...].astype(F32).reshape(rows, D_GROUP)
    for j in range(ntile):
        perm_s[j] = u_seq[:, j * LANES:(j + 1) * LANES]

    def to_frame_major(g, carry):
        dst = pl.ds(pl.multiple_of(g * nseq, nseq), nseq)
        for j in range(ntile):
            up_s[dst, j * LANES:(j + 1) * LANES] = perm_s[j, pl.ds(g, nseq, stride=T), :]
        return carry

    lax.fori_loop(0, T, to_frame_major, 0, unroll=8)
    u = up_s[...]
    ub = u.astype(BF16)
    for blk in range(S5_BLOCKS):
        cols = slice(blk * NS, (blk + 1) * NS)
        bu = _dot(ub[:, blk * 128:(blk + 1) * 128], bd_ref[blk])
        xr_s[:, cols] = bu[:, :NS]
        xi_s[:, cols] = bu[:, NS:]

    for blk in range(S5_BLOCKS):
        cols = slice(blk * NS, (blk + 1) * NS)
        ar = are_ref[:, cols]
        ai = aim_ref[:, cols]

        def body(g, carry, cols=cols, ar=ar, ai=ai):
            xr, xi = carry
            frame = pl.ds(pl.multiple_of(g * nseq, nseq), nseq)
            nr = ar * xr - ai * xi + xr_s[frame, cols]
            ni = ar * xi + ai * xr + xi_s[frame, cols]
            xr_s[frame, cols] = nr
            xi_s[frame, cols] = ni
            return nr, ni

        cr, ci = lax.fori_loop(0, T, body, (car_s[:, cols], cai_s[:, cols]), unroll=4)
        car_s[:, cols] = cr
        cai_s[:, cols] = ci

    ys = []
    for blk in range(S5_BLOCKS):
        cols = slice(blk * NS, (blk + 1) * NS)
        ys.append(_dot(xr_s[:, cols].astype(BF16), cd_ref[blk, :NS, :])
                  + _dot(xi_s[:, cols].astype(BF16), cd_ref[blk, NS:, :]))
    y = _gelu_tanh(jnp.concatenate(ys, axis=1) + dskip_ref[...] * u)
    y = y * _sigmoid(_dot(y.astype(BF16), wglu_ref[...]) + bglu_ref[...])
    for j in range(ntile):
        perm_s[j] = y[:, j * LANES:(j + 1) * LANES]
    for i in range(nseq):
        y_i = jnp.concatenate([perm_s[j, pl.ds(i, T, stride=nseq), :] for j in range(ntile)], axis=1)
        y_ref[i] = (_silu(z_ref[i].astype(F32)) * y_i).astype(y_ref.dtype)
    sre_ref[...] = car_s[...]
    sim_ref[...] = cai_s[...]


def _s5(u, bd, cd, a_re, a_im, dskip, wglu, bglu, s0_re, s0_im, batch, seq, T):
    nseq = SUBLANES
    nt = seq // T
    ns_all = S5_GROUPS * S5_STATE
    u3 = u.reshape(batch, seq, N_MAIN)

    def tok(colblk):
        return pl.BlockSpec((nseq, T, D_GROUP), lambda b, t: (b, t, colblk))

    def full(shape):
        return pl.BlockSpec(shape, lambda b, t: (0,) * len(shape))

    st = pl.BlockSpec((nseq, ns_all), lambda b, t: (b, 0))
    y, s_re, s_im = pl.pallas_call(
        _s5_kernel,
        grid=(batch // nseq, nt),
        in_specs=[tok(COL_SU), tok(COL_SZ),
                  full(bd.shape), full(cd.shape), full(a_re.shape), full(a_im.shape),
                  full(dskip.shape), full(wglu.shape), full(bglu.shape), st, st],
        out_specs=[pl.BlockSpec((nseq, T, D_GROUP), lambda b, t: (b, t, 0)), st, st],
        out_shape=[jax.ShapeDtypeStruct((batch, seq, D_GROUP), BF16),
                   jax.ShapeDtypeStruct((batch, ns_all), F32),
                   jax.ShapeDtypeStruct((batch, ns_all), F32)],
        scratch_shapes=[pltpu.VMEM((D_GROUP // LANES, nseq * T, LANES), F32),
                        pltpu.VMEM((nseq * T, D_GROUP), F32),
                        pltpu.VMEM((nseq * T, ns_all), F32), pltpu.VMEM((nseq * T, ns_all), F32),
                        pltpu.VMEM((nseq, ns_all), F32), pltpu.VMEM((nseq, ns_all), F32)],
        compiler_params=_params(2),
        name="s5",
    )(u3, u3, bd, cd, a_re, a_im, dskip, wglu, bglu, s0_re, s0_im)
    return y.reshape(batch * seq, D_GROUP), s_re, s_im


def _rel_bias_table(rel_bias, n_q, n_k, offset, band):
    j = jnp.arange(n_k + n_q - 1)
    idx = jnp.clip(offset + n_q - 1 - j, -REL_CLIP, REL_CLIP) + REL_CLIP
    ext = rel_bias[..., idx].astype(F32)
    tbl = jnp.stack([ext[..., n_q - 1 - i:n_q - 1 - i + n_k] for i in range(n_q)], axis=-2)
    if band:
        qc = jnp.arange(n_q)[:, None] // CHUNK
        kc = jnp.arange(n_k)[None, :] // CHUNK
        tbl = jnp.where((kc >= qc) & (kc <= qc + BAND), tbl, -jnp.inf)
    return tbl


def _s5_params(a_re, a_im, log_dt, b_re, b_im, c_re, c_im):
    dt = jnp.exp(log_dt)[:, None]
    mag = jnp.exp(a_re * dt)
    ab_re, ab_im = mag * jnp.cos(a_im * dt), mag * jnp.sin(a_im * dt)
    den = a_re * a_re + a_im * a_im
    co_re = ((ab_re - 1.0) * a_re + ab_im * a_im) / den
    co_im = (ab_im * a_re - (ab_re - 1.0) * a_im) / den
    bb_re = co_re[..., None] * b_re - co_im[..., None] * b_im
    bb_im = co_re[..., None] * b_im + co_im[..., None] * b_re
    a8_re = jnp.broadcast_to(ab_re.reshape(1, -1), (SUBLANES, S5_GROUPS * S5_STATE))
    a8_im = jnp.broadcast_to(ab_im.reshape(1, -1), (SUBLANES, S5_GROUPS * S5_STATE))
    gpb = S5_GROUPS // S5_BLOCKS
    eye = jnp.eye(gpb, dtype=F32)

    def blockdiag_in(bb):
        bbk = bb.reshape(S5_BLOCKS, gpb, S5_STATE, S5_CH)
        return jnp.einsum('kgpc,gh->kgchp', bbk, eye).reshape(S5_BLOCKS, gpb * S5_CH, gpb * S5_STATE)

    def blockdiag_out(cc):
        cck = cc.reshape(S5_BLOCKS, gpb, S5_CH, S5_STATE)
        return jnp.einsum('kgcp,gh->kgphc', cck, eye).reshape(S5_BLOCKS, gpb * S5_STATE, gpb * S5_CH)

    bd = jnp.concatenate([blockdiag_in(bb_re), blockdiag_in(bb_im)], axis=-1).astype(BF16)
    cd = jnp.concatenate([blockdiag_out(c_re), blockdiag_out(-c_im)], axis=1).astype(BF16)
    return bd, cd, a8_re, a8_im


def _layer(x, xn, batch, seq, att_cache, ml_state, conv_state, s5_state, p, layer, last, tiles):
    tm_in, tn_in, tm_out, L, t_conv, t_s5 = tiles
    u, gates = _inproj(xn, p["w_in_all"], layer, tm_in, tn_in)
    nc = seq // L
    gates_t = gates[:, :SUBLANES].reshape(batch, nc, L, SUBLANES).transpose(0, 1, 3, 2)

    if att_cache is None:
        ya = _attn_prompt(u, p["bias"], batch, seq)
        rows = min(BAND * CHUNK, seq)
        u3 = u.reshape(batch, seq, N_MAIN)
        new_k = u3[:, seq - rows:, COL_AK * D_GROUP:(COL_AK + 1) * D_GROUP].astype(F32)
        new_v = u3[:, seq - rows:, COL_AV * D_GROUP:(COL_AV + 1) * D_GROUP].astype(F32)
    else:
        ya, new_k, new_v = _attn_sample(u, att_cache[0], att_cache[1], p["bias"], batch, seq)
    new_k = new_k.reshape(batch, -1, ATT_HEADS, ATT_DH)
    new_v = new_v.reshape(batch, -1, ATT_HEADS, ATT_DH)

    c0, n0, m0 = ml_state
    m0 = jnp.broadcast_to(jnp.pad(m0, ((0, 0), (0, SUBLANES - ML_HEADS)))[:, :, None],
                          (batch, SUBLANES, ML_DH))
    ym, c_new, n_new, m_new = _mlstm(u, gates, gates_t, p["b_col"], p["b_row"], c0, n0, m0, batch, seq, L)
    m_new = m_new[:, :ML_HEADS, 0]

    yc, conv_new = _conv(u, p["conv_w"], conv_state, batch, seq, t_conv)

    ys, s5_re, s5_im = _s5(u, p["bd"], p["cd"], p["a_re"], p["a_im"], p["dskip"], p["wglu"], p["bglu"],
                           s5_state[0].reshape(batch, -1), s5_state[1].reshape(batch, -1),
                           batch, seq, t_s5)
    s5_re = s5_re.reshape(batch, S5_GROUPS, S5_STATE)
    s5_im = s5_im.reshape(batch, S5_GROUPS, S5_STATE)

    outs = _outproj(ya, ym, yc, ys, x, p["w_out_all"], layer, p["g_next"], last, tm_out)
    return outs, (new_k, new_v, c_new, n_new, m_new, conv_new, s5_re, s5_im)


def kernel(x_prompt, x_sample, cache_attn_k, cache_attn_v, state_mlstm_C, state_mlstm_n, state_mlstm_m, state_conv, state_s5_re, state_s5_im, norm_g, w_in, w_out, attn_rel_bias, mlstm_b_if, conv_w, s5_A_re, s5_A_im, s5_log_dt, s5_B_re, s5_B_im, s5_C_re, s5_C_im, s5_D, s5_w_glu, s5_b_glu, final_norm_g):
    depth = w_in.shape[0]
    bp, sp, _ = x_prompt.shape
    bs, ss, _ = x_sample.shape
    w_rows = cache_attn_k.shape[2]
    gate_lo = 9 * D_GROUP
    gate_hi = gate_lo + 2 * ML_HEADS

    xp = x_prompt.reshape(bp * sp, D_MODEL)
    xs = x_sample.reshape(bs * ss, D_MODEL)
    ml0 = (jnp.zeros((bp, ML_HEADS, ML_DH, ML_DH), F32), jnp.zeros((bp, ML_HEADS, ML_DH), F32),
           jnp.zeros((bp, ML_HEADS), F32))
    conv0 = jnp.zeros((bp, CONV_W - 1, D_GROUP), F32)
    s50 = (jnp.zeros((bp, S5_GROUPS, S5_STATE), F32), jnp.zeros((bp, S5_GROUPS, S5_STATE), F32))

    tiles_p = (1024, 1536, 512, 4 * CHUNK, 512, 128)
    tiles_s = (bs * ss, 1536, bs * ss, ss, ss, ss)
    att_rows = 4 * CHUNK
    bias_p = _rel_bias_table(attn_rel_bias, att_rows, BAND * CHUNK + att_rows, BAND * CHUNK, True)
    bias_s = _rel_bias_table(attn_rel_bias, ss, w_rows + ss, w_rows, False)
    w_in_all = jnp.concatenate(
        [w_in[:, :, :gate_lo], w_in[:, :, gate_hi:],
         jnp.pad(w_in[:, :, gate_lo:gate_hi], ((0, 0), (0, 0), (0, GATE_PAD - 2 * ML_HEADS)))],
        axis=2).astype(BF16)
    w_out_all = w_out.astype(BF16)
    g_all = jnp.concatenate([norm_g, final_norm_g[None]], axis=0).reshape(depth + 1, 1, D_MODEL)
    xnp = _prenorm(xp, g_all[0], 512)
    xns = _prenorm(xs, g_all[0], bs * ss)

    st_p, st_s = [], []
    for l in range(depth):
        bd, cd, a8_re, a8_im = _s5_params(s5_A_re[l], s5_A_im[l], s5_log_dt[l], s5_B_re[l], s5_B_im[l],
                                          s5_C_re[l], s5_C_im[l])
        b_if = mlstm_b_if[l]
        last = l == depth - 1
        p = {
            "w_in_all": w_in_all, "w_out_all": w_out_all, "g_next": g_all[l + 1],
            "b_col": jnp.pad(b_if, (0, GATE_PAD - 2 * ML_HEADS)).reshape(1, GATE_PAD),
            "b_row": b_if.reshape(SUBLANES, 1),
            "conv_w": conv_w[l],
            "bd": bd, "cd": cd, "a_re": a8_re, "a_im": a8_im,
            "dskip": s5_D[l].reshape(1, D_GROUP),
            "wglu": s5_w_glu[l].astype(BF16),
            "bglu": s5_b_glu[l].reshape(1, D_GROUP),
        }
        p_prompt = dict(p, bias=bias_p[l])
        p_sample = dict(p, bias=bias_s[l])

        out_p, sp_l = _layer(xp, xnp, bp, sp, None, ml0, conv0, s50, p_prompt, l, last, tiles_p)
        out_s, ss_l = _layer(xs, xns, bs, ss,
                             (cache_attn_k[l].reshape(bs, w_rows, D_GROUP),
                              cache_attn_v[l].reshape(bs, w_rows, D_GROUP)),
                             (state_mlstm_C[l], state_mlstm_n[l], state_mlstm_m[l]),
                             state_conv[l], (state_s5_re[l], state_s5_im[l]), p_sample, l, last, tiles_s)
        st_p.append(sp_l)
        st_s.append(ss_l)
        if not last:
            (xp, xnp), (xs, xns) = out_p, out_s

    y_prompt = out_p[0].reshape(bp, sp, D_MODEL)
    y_sample = out_s[0].reshape(bs, ss, D_MODEL)
    outs_p = [jnp.stack(t) for t in zip(*st_p)]
    outs_s = [jnp.stack(t) for t in zip(*st_s)]
    return (y_prompt, y_sample, *outs_p, *outs_s)
```

```python
import functools
import math

import jax
import jax.numpy as jnp
from jax import lax
from jax.experimental import pallas as pl
from jax.experimental.pallas import tpu as pltpu

F32 = jnp.float32
BF16 = jnp.bfloat16

D_MODEL = 2048
D_GROUP = 512
CHUNK = 64
BAND = 8
REL_CLIP = 128
ATT_HEADS = 8
ATT_DH = 64
ML_HEADS = 4
ML_DH = 128
CONV_W = 3
S5_GROUPS = 32
S5_CH = 16
S5_STATE = 64
S5_BLOCKS = 4
S5_BLK_STATES = S5_GROUPS * S5_STATE // S5_BLOCKS
NORM_EPS = 1e-6
N_MAIN = 15 * D_GROUP
GATE_PAD = 128
SUBLANES = 8
LANES = 128
IN_PROJ_ROWS = 1024
IN_PROJ_COLS = 3 * D_GROUP
OUT_PROJ_ROWS = 512
MLSTM_CHUNK = 4 * CHUNK
S5_FRAMES = 128
ATT_CHUNKS_PER_STEP = 4
VMEM_LIMIT = 56 * 1024 * 1024

COL_AQ, COL_AK, COL_AV, COL_AZ = 0, 1, 2, 3
COL_MQ, COL_MK, COL_MV, COL_MO, COL_MZ = 4, 5, 6, 7, 8
COL_CB, COL_CC, COL_CX, COL_CZ = 9, 10, 11, 12
COL_SU, COL_SZ = 13, 14


def _params(n_axes):
    return pltpu.CompilerParams(dimension_semantics=("arbitrary",) * n_axes,
                                vmem_limit_bytes=VMEM_LIMIT)


def _sigmoid(x):
    return 1.0 / (1.0 + jnp.exp(-x))


def _silu(x):
    return x * _sigmoid(x)


def _log_sigmoid(x):
    return jnp.minimum(x, 0.0) - jnp.log(1.0 + jnp.exp(-jnp.abs(x)))


def _gelu_tanh(x):
    return 0.5 * x * (1.0 + jnp.tanh(math.sqrt(2.0 / math.pi) * (x + 0.044715 * (x * x * x))))


def _rms(x, g):
    return x * lax.rsqrt(jnp.mean(x * x, axis=-1, keepdims=True) + NORM_EPS) * g


def _dot(a, b):
    return jnp.dot(a, b, preferred_element_type=F32)


def _dot_nt(a, b):
    return lax.dot_general(a, b, (((1,), (1,)), ((), ())), preferred_element_type=F32)


def _dot_tn(a, b):
    return lax.dot_general(a, b, (((0,), (0,)), ((), ())), preferred_element_type=F32)


def _dot_exact(a, b):
    return jnp.dot(a, b, preferred_element_type=F32, precision=lax.Precision.HIGHEST)


def _prenorm_kernel(x_ref, g_ref, o_ref):
    o_ref[...] = _rms(x_ref[...], g_ref[...]).astype(o_ref.dtype)


def _prenorm(x, g, tm):
    n = x.shape[0]
    return pl.pallas_call(
        _prenorm_kernel,
        grid=(n // tm,),
        in_specs=[pl.BlockSpec((tm, D_MODEL), lambda i: (i, 0)),
                  pl.BlockSpec((1, D_MODEL), lambda i: (0, 0))],
        out_specs=pl.BlockSpec((tm, D_MODEL), lambda i: (i, 0)),
        out_shape=jax.ShapeDtypeStruct((n, D_MODEL), BF16),
        compiler_params=_params(1),
        name="prenorm",
    )(x, g)


def _inproj_kernel(xn_ref, w_ref, wg_ref, u_ref, gate_ref):
    @pl.when(pl.program_id(1) == 0)
    def _():
        gate_ref[...] = _dot_nt(xn_ref[...], wg_ref[...])

    u_ref[...] = _dot_nt(xn_ref[...], w_ref[...]).astype(u_ref.dtype)


def _inproj(xn, wt_all, layer, tm, tn):
    n = xn.shape[0]
    return pl.pallas_call(
        _inproj_kernel,
        grid=(n // tm, N_MAIN // tn),
        in_specs=[
            pl.BlockSpec((tm, D_MODEL), lambda i, j: (i, 0)),
            pl.BlockSpec((None, tn, D_MODEL), lambda i, j: (layer, j, 0)),
            pl.BlockSpec((None, GATE_PAD, D_MODEL), lambda i, j: (layer, N_MAIN // GATE_PAD, 0)),
        ],
        out_specs=[
            pl.BlockSpec((tm, tn), lambda i, j: (i, j)),
            pl.BlockSpec((tm, GATE_PAD), lambda i, j: (i, 0)),
        ],
        out_shape=[
            jax.ShapeDtypeStruct((n, N_MAIN), BF16),
            jax.ShapeDtypeStruct((n, GATE_PAD), F32),
        ],
        compiler_params=_params(2),
        name="inproj",
    )(xn, wt_all, wt_all)


def _outproj_kernel(ya_ref, ym_ref, ys_ref, cb_ref, cc_ref, cx_ref, cz_ref, cw_ref, cs0_ref,
                    x_ref, w_ref, g_ref, *refs, tiles_per_seq):
    *out_refs, cs_ref, u_s = refs
    tm = x_ref.shape[0]
    head = SUBLANES
    taps = CONV_W - 1

    @pl.when(pl.program_id(0) % tiles_per_seq == 0)
    def _():
        u_s[head - taps:head, :] = cs0_ref[0]

    u_s[head:, :] = cc_ref[...].astype(F32) * cx_ref[...].astype(F32)
    y = cw_ref[0:1, :] * u_s[head - 2:head - 2 + tm, :]
    y = y + cw_ref[1:2, :] * u_s[head - 1:head - 1 + tm, :]
    y = y + cw_ref[2:3, :] * u_s[head:, :]
    yc = (_silu(cz_ref[...].astype(F32)) * (cb_ref[...].astype(F32) * y)).astype(BF16)
    last_rows = u_s[head + tm - taps:, :]
    cs_ref[0] = last_rows
    u_s[head - taps:head, :] = last_rows

    acc = x_ref[...]
    for k, yk in enumerate((ya_ref[...], ym_ref[...], yc, ys_ref[...])):
        acc = acc + _dot(yk, w_ref[k * D_GROUP:(k + 1) * D_GROUP, :])
    normed = _rms(acc, g_ref[...])
    if len(out_refs) == 2:
        out_refs[0][...] = acc
    out_refs[-1][...] = normed.astype(out_refs[-1].dtype)


def _outproj(ya, ym, ys, u, conv_w, conv_s0, x, w_out_all, layer, g_next, last, batch, seq, tm):
    n = x.shape[0]
    tps = seq // tm
    yspec = pl.BlockSpec((tm, D_GROUP), lambda i: (i, 0))
    row = pl.BlockSpec((tm, D_MODEL), lambda i: (i, 0))
    st = pl.BlockSpec((1, CONV_W - 1, D_GROUP), lambda i: (i // tps, 0, 0))

    def tok(colblk):
        return pl.BlockSpec((tm, D_GROUP), lambda i: (i, colblk))

    out_specs, out_shape = [row], [jax.ShapeDtypeStruct((n, D_MODEL), F32)]
    if not last:
        out_specs.append(row)
        out_shape.append(jax.ShapeDtypeStruct((n, D_MODEL), BF16))
    out_specs.append(st)
    out_shape.append(jax.ShapeDtypeStruct((batch, CONV_W - 1, D_GROUP), F32))
    return pl.pallas_call(
        functools.partial(_outproj_kernel, tiles_per_seq=tps),
        grid=(n // tm,),
        in_specs=[yspec, yspec, yspec, tok(COL_CB), tok(COL_CC), tok(COL_CX), tok(COL_CZ),
                  pl.BlockSpec((CONV_W, D_GROUP), lambda i: (0, 0)), st, row,
                  pl.BlockSpec((None, D_MODEL, D_MODEL), lambda i: (layer, 0, 0)),
                  pl.BlockSpec((1, D_MODEL), lambda i: (0, 0))],
        out_specs=out_specs,
        out_shape=out_shape,
        scratch_shapes=[pltpu.VMEM((SUBLANES + tm, D_GROUP), F32)],
        compiler_params=_params(1),
        name="outproj",
    )(ya, ym, ys, u, u, u, u, conv_w, conv_s0, x, w_out_all, g_next)


def _attn_heads(q, z, k_all, v_all, bias_ref, first_key, o_ref):
    lq, nk = q.shape[0], k_all.shape[0]
    pair_w = 2 * ATT_DH
    first = lax.broadcasted_iota(jnp.int32, (lq, pair_w), 1) < ATT_DH
    kcol = lax.broadcasted_iota(jnp.int32, (2 * lq, nk), 1)
    ones = jnp.ones((nk, pair_w), BF16)
    for pair in range(ATT_HEADS // 2):
        sl = slice(pair * pair_w, (pair + 1) * pair_w)
        qp = q[:, sl] * (ATT_DH ** -0.5)
        kp, vp = k_all[:, sl], v_all[:, sl]
        q2 = jnp.concatenate([jnp.where(first, qp, 0.0), jnp.where(first, 0.0, qp)], axis=0).astype(BF16)
        s = _dot_nt(q2, kp) + jnp.concatenate([bias_ref[2 * pair], bias_ref[2 * pair + 1]], axis=0)
        if first_key is not None:
            s = jnp.where(kcol >= first_key, s, -jnp.inf)
        p = jnp.exp(s - jnp.max(s, axis=-1, keepdims=True)).astype(BF16)
        ol = _dot(p, jnp.concatenate([vp, ones], axis=1))
        on = ol[:, :pair_w] * (1.0 / ol[:, pair_w:])
        o = jnp.where(first, on[:lq], on[lq:])
        o_ref[:, sl] = (_silu(z[:, sl]) * o).astype(o_ref.dtype)


def _attn_prompt_kernel(q_ref, k_ref, v_ref, z_ref, bias_ref, o_ref, kpad_ref, vpad_ref):
    step = pl.program_id(1)
    pad = BAND * CHUNK
    rows = q_ref.shape[0]
    nk = pad + rows

    @pl.when(step == 0)
    def _():
        kpad_ref[0:pad, :] = jnp.zeros((pad, D_GROUP), BF16)
        vpad_ref[0:pad, :] = jnp.zeros((pad, D_GROUP), BF16)
        kpad_ref[pad:, :] = k_ref[...].astype(BF16)
        vpad_ref[pad:, :] = v_ref[...].astype(BF16)

    start = pl.multiple_of(step * rows, rows)
    k_all = kpad_ref[pl.ds(start, nk), :]
    v_all = vpad_ref[pl.ds(start, nk), :]
    _attn_heads(q_ref[...].astype(F32), z_ref[...].astype(F32), k_all, v_all, bias_ref,
                pad - step * rows, o_ref)


def _attn_prompt(u, bias, batch, seq):
    rows, nk = bias.shape[1], bias.shape[2]
    ns = seq // rows
    return pl.pallas_call(
        _attn_prompt_kernel,
        grid=(batch, ns),
        in_specs=[
            pl.BlockSpec((rows, D_GROUP), lambda b, s: (b * ns + s, COL_AQ)),
            pl.BlockSpec((seq, D_GROUP), lambda b, s: (b, COL_AK)),
            pl.BlockSpec((seq, D_GROUP), lambda b, s: (b, COL_AV)),
            pl.BlockSpec((rows, D_GROUP), lambda b, s: (b * ns + s, COL_AZ)),
            pl.BlockSpec((ATT_HEADS, rows, nk), lambda b, s: (0, 0, 0)),
        ],
        out_specs=pl.BlockSpec((rows, D_GROUP), lambda b, s: (b * ns + s, 0)),
        out_shape=jax.ShapeDtypeStruct((batch * seq, D_GROUP), BF16),
        scratch_shapes=[pltpu.VMEM((seq + BAND * CHUNK, D_GROUP), BF16),
                        pltpu.VMEM((seq + BAND * CHUNK, D_GROUP), BF16)],
        compiler_params=_params(2),
        name="attn_prompt",
    )(u, u, u, u, bias)


def _attn_sample_kernel(q_ref, k_ref, v_ref, z_ref, kc_ref, vc_ref, bias_ref,
                        o_ref, kn_ref, vn_ref, kall_ref, vall_ref):
    w = kc_ref.shape[1]
    t = q_ref.shape[0]
    kall_ref[0:w, :] = kc_ref[0].astype(BF16)
    vall_ref[0:w, :] = vc_ref[0].astype(BF16)
    kall_ref[w:, :] = k_ref[...].astype(BF16)
    vall_ref[w:, :] = v_ref[...].astype(BF16)
    kn_ref[0, 0:w - t, :] = kc_ref[0, t:, :]
    vn_ref[0, 0:w - t, :] = vc_ref[0, t:, :]
    kn_ref[0, w - t:, :] = k_ref[...].astype(F32)
    vn_ref[0, w - t:, :] = v_ref[...].astype(F32)
    _attn_heads(q_ref[...].astype(F32), z_ref[...].astype(F32), kall_ref[...], vall_ref[...], bias_ref, None, o_ref)


def _attn_sample(u, k_cache_all, v_cache_all, layer, bias, batch, seq):
    w = k_cache_all.shape[2]
    tok = pl.BlockSpec((seq, D_GROUP), lambda b: (b, 0))
    cache = pl.BlockSpec((1, w, D_GROUP), lambda b: (b, 0, 0))
    cache_in = pl.BlockSpec((None, 1, w, D_GROUP), lambda b: (layer, b, 0, 0))
    return pl.pallas_call(
        _attn_sample_kernel,
        grid=(batch,),
        in_specs=[
            pl.BlockSpec((seq, D_GROUP), lambda b: (b, COL_AQ)),
            pl.BlockSpec((seq, D_GROUP), lambda b: (b, COL_AK)),
            pl.BlockSpec((seq, D_GROUP), lambda b: (b, COL_AV)),
            pl.BlockSpec((seq, D_GROUP), lambda b: (b, COL_AZ)),
            cache_in, cache_in,
            pl.BlockSpec((ATT_HEADS, seq, w + seq), lambda b: (0, 0, 0)),
        ],
        out_specs=[tok, cache, cache],
        out_shape=[jax.ShapeDtypeStruct((batch * seq, D_GROUP), BF16),
                   jax.ShapeDtypeStruct((batch, w, D_GROUP), F32),
                   jax.ShapeDtypeStruct((batch, w, D_GROUP), F32)],
        scratch_shapes=[pltpu.VMEM((w + seq, D_GROUP), BF16),
                        pltpu.VMEM((w + seq, D_GROUP), BF16)],
        compiler_params=_params(1),
        name="attn_sample",
    )(u, u, u, u, k_cache_all, v_cache_all, bias)


def _mlstm_kernel(q_ref, k_ref, v_ref, o_ref, z_ref, gc_ref, gr_ref, bc_ref, br_ref,
                  c0_ref, n0_ref, m0_ref,
                  y_ref, cout_ref, nout_ref, mout_ref, c_s, n_s, m_s):
    c = pl.program_id(1)
    L = q_ref.shape[0]

    @pl.when(c == 0)
    def _():
        c_s[...] = c0_ref[0]
        n_s[...] = n0_ref[0]
        m_s[...] = m0_ref[0]

    row = lax.broadcasted_iota(jnp.int32, (L, L), 0)
    col = lax.broadcasted_iota(jnp.int32, (L, L), 1)
    tril = (row >= col).astype(F32)
    triu = (row <= col).astype(F32)
    causal = row >= col

    g_col = gc_ref[...] + bc_ref[...]
    g_row = gr_ref[0, 0] + br_ref[...]
    b_col = _dot_exact(tril, _log_sigmoid(g_col))
    b_row = _dot_exact(_log_sigmoid(g_row), triu)

    m_all, n_all = m_s[...], n_s[...]
    c_all = [c_s[h] for h in range(ML_HEADS)]
    c_new, n_new, m_new_rows = [], [], []

    for h in range(ML_HEADS):
        sl = slice(h * ML_DH, (h + 1) * ML_DH)
        bt = b_col[:, ML_HEADS + h:ML_HEADS + h + 1]
        ig_t = g_col[:, h:h + 1]
        bs = b_row[ML_HEADS + h:ML_HEADS + h + 1, :]
        ig_s = g_row[h:h + 1, :]
        m_prev = m_all[h:h + 1, 0:1]
        n_prev = n_all[h:h + 1, :]
        c_prev = c_all[h]

        qh = q_ref[:, sl].astype(F32)
        kh = k_ref[:, sl].astype(F32) * (ML_DH ** -0.5)
        vh = v_ref[:, sl].astype(F32)
        qb, kb = qh.astype(BF16), kh.astype(BF16)

        d = jnp.where(causal, bt - bs + ig_s, -jnp.inf)
        m_inter = bt + m_prev
        m_t = jnp.maximum(m_inter, jnp.max(d, axis=-1, keepdims=True))
        w_intra = jnp.exp(d - m_t) * _dot_nt(qb, kb)
        w_inter = jnp.exp(m_inter - m_t)
        num = w_inter * _dot_nt(qb, c_prev.astype(BF16)) + _dot(w_intra.astype(BF16), vh.astype(BF16))
        den = (w_inter * jnp.sum(qh * n_prev, axis=-1, keepdims=True)
               + jnp.sum(w_intra, axis=-1, keepdims=True))
        hh = num / jnp.maximum(jnp.abs(den), jnp.exp(-m_t))
        ym = _sigmoid(o_ref[:, sl].astype(F32)) * hh
        y_ref[:, sl] = (_silu(z_ref[:, sl].astype(F32)) * ym).astype(y_ref.dtype)

        b_last = bt[L - 1:L, :]
        m_new = jnp.maximum(b_last + m_prev, jnp.max(b_last - bs + ig_s, axis=-1, keepdims=True))
        ws = jnp.exp(b_last - bt + ig_t - m_new)
        decay = jnp.exp(b_last + m_prev - m_new)
        c_new.append(decay * c_prev + _dot_tn((vh * ws).astype(BF16), kb))
        n_new.append(decay * n_prev + jnp.sum(ws * kh, axis=0, keepdims=True))
        m_new_rows.append(jnp.broadcast_to(m_new, (1, m_s.shape[1])))

    for h in range(ML_HEADS):
        c_s[h] = c_new[h]
        cout_ref[0, h] = c_new[h]
    n_cat = jnp.concatenate(n_new, axis=0)
    n_s[...] = n_cat
    nout_ref[0] = n_cat
    m_cat = jnp.concatenate(m_new_rows + [m_all[ML_HEADS:, :]], axis=0)
    m_s[...] = m_cat
    mout_ref[0] = m_cat


def _mlstm(u, gates, gates_t, b_col, b_row, c0, n0, m0, batch, seq, L):
    nc = seq // L

    def tok(colblk):
        return pl.BlockSpec((L, D_GROUP), lambda b, c: (b * nc + c, colblk))

    st_c = pl.BlockSpec((1, ML_HEADS, ML_DH, ML_DH), lambda b, c: (b, 0, 0, 0))
    st_n = pl.BlockSpec((1, ML_HEADS, ML_DH), lambda b, c: (b, 0, 0))
    st_m = pl.BlockSpec((1, SUBLANES, ML_DH), lambda b, c: (b, 0, 0))
    return pl.pallas_call(
        _mlstm_kernel,
        grid=(batch, nc),
        in_specs=[
            tok(COL_MQ), tok(COL_MK), tok(COL_MV), tok(COL_MO), tok(COL_MZ),
            pl.BlockSpec((L, GATE_PAD), lambda b, c: (b * nc + c, 0)),
            pl.BlockSpec((1, 1, SUBLANES, L), lambda b, c: (b, c, 0, 0)),
            pl.BlockSpec((1, GATE_PAD), lambda b, c: (0, 0)),
            pl.BlockSpec((SUBLANES, 1), lambda b, c: (0, 0)),
            st_c, st_n, st_m,
        ],
        out_specs=[pl.BlockSpec((L, D_GROUP), lambda b, c: (b * nc + c, 0)), st_c, st_n, st_m],
        out_shape=[jax.ShapeDtypeStruct((batch * seq, D_GROUP), BF16),
                   jax.ShapeDtypeStruct((batch, ML_HEADS, ML_DH, ML_DH), F32),
                   jax.ShapeDtypeStruct((batch, ML_HEADS, ML_DH), F32),
                   jax.ShapeDtypeStruct((batch, SUBLANES, ML_DH), F32)],
        scratch_shapes=[pltpu.VMEM((ML_HEADS, ML_DH, ML_DH), F32),
                        pltpu.VMEM((ML_HEADS, ML_DH), F32),
                        pltpu.VMEM((SUBLANES, ML_DH), F32)],
        compiler_params=_params(2),
        name="mlstm",
    )(u, u, u, u, u, gates, gates_t, b_col, b_row, c0, n0, m0)


def _s5_kernel(u_ref, z_ref, bd_ref, cd_ref, are_ref, aim_ref, dskip_ref, wglu_ref, bglu_ref,
               s0re_ref, s0im_ref, y_ref, sre_ref, sim_ref, perm_s, up_s, xr_s, xi_s, car_s, cai_s):
    t = pl.program_id(1)
    nseq, T = u_ref.shape[0], u_ref.shape[1]
    rows = nseq * T
    NS = S5_BLK_STATES
    ntile = D_GROUP // LANES

    @pl.when(t == 0)
    def _():
        car_s[...] = s0re_ref[...]
        cai_s[...] = s0im_ref[...]

    u_seq = u_ref[...].astype(F32).reshape(rows, D_GROUP)
    for j in range(ntile):
        perm_s[j] = u_seq[:, j * LANES:(j + 1) * LANES]

    def to_frame_major(g, carry):
        dst = pl.ds(pl.multiple_of(g * nseq, nseq), nseq)
        for j in range(ntile):
            up_s[dst, j * LANES:(j + 1) * LANES] = perm_s[j, pl.ds(g, nseq, stride=T), :]
        return carry

    lax.fori_loop(0, T, to_frame_major, 0, unroll=8)
    u = up_s[...]
    ub = u.astype(BF16)
    for blk in range(S5_BLOCKS):
        cols = slice(blk * NS, (blk + 1) * NS)
        bu = _dot(ub[:, blk * 128:(blk + 1) * 128], bd_ref[blk])
        xr_s[:, cols] = bu[:, :NS]
        xi_s[:, cols] = bu[:, NS:]

    for blk in range(S5_BLOCKS):
        cols = slice(blk * NS, (blk + 1) * NS)
        ar = are_ref[:, cols]
        ai = aim_ref[:, cols]

        def body(g, carry, cols=cols, ar=ar, ai=ai):
            xr, xi = carry
            frame = pl.ds(pl.multiple_of(g * nseq, nseq), nseq)
            nr = ar * xr - ai * xi + xr_s[frame, cols]
            ni = ar * xi + ai * xr + xi_s[frame, cols]
            xr_s[frame, cols] = nr
            xi_s[frame, cols] = ni
            return nr, ni

        cr, ci = lax.fori_loop(0, T, body, (car_s[:, cols], cai_s[:, cols]), unroll=4)
        car_s[:, cols] = cr
        cai_s[:, cols] = ci

    ys = []
    for blk in range(S5_BLOCKS):
        cols = slice(blk * NS, (blk + 1) * NS)
        ys.append(_dot(xr_s[:, cols].astype(BF16), cd_ref[blk, :NS, :])
                  + _dot(xi_s[:, cols].astype(BF16), cd_ref[blk, NS:, :]))
    y = _gelu_tanh(jnp.concatenate(ys, axis=1) + dskip_ref[...] * u)
    y = y * _sigmoid(_dot(y.astype(BF16), wglu_ref[...]) + bglu_ref[...])
    for j in range(ntile):
        perm_s[j] = y[:, j * LANES:(j + 1) * LANES]
    for i in range(nseq):
        y_i = jnp.concatenate([perm_s[j, pl.ds(i, T, stride=nseq), :] for j in range(ntile)], axis=1)
        y_ref[i] = (_silu(z_ref[i].astype(F32)) * y_i).astype(y_ref.dtype)
    sre_ref[...] = car_s[...]
    sim_ref[...] = cai_s[...]


def _s5(u, bd, cd, a_re, a_im, dskip, wglu, bglu, s0_re, s0_im, batch, seq, T):
    nseq = SUBLANES
    nt = seq // T
    ns_all = S5_GROUPS * S5_STATE
    u3 = u.reshape(batch, seq, N_MAIN)

    def tok(colblk):
        return pl.BlockSpec((nseq, T, D_GROUP), lambda b, t: (b, t, colblk))

    def full(shape):
        return pl.BlockSpec(shape, lambda b, t: (0,) * len(shape))

    st = pl.BlockSpec((nseq, ns_all), lambda b, t: (b, 0))
    y, s_re, s_im = pl.pallas_call(
        _s5_kernel,
        grid=(batch // nseq, nt),
        in_specs=[tok(COL_SU), tok(COL_SZ),
                  full(bd.shape), full(cd.shape), full(a_re.shape), full(a_im.shape),
                  full(dskip.shape), full(wglu.shape), full(bglu.shape), st, st],
        out_specs=[pl.BlockSpec((nseq, T, D_GROUP), lambda b, t: (b, t, 0)), st, st],
        out_shape=[jax.ShapeDtypeStruct((batch, seq, D_GROUP), BF16),
                   jax.ShapeDtypeStruct((batch, ns_all), F32),
                   jax.ShapeDtypeStruct((batch, ns_all), F32)],
        scratch_shapes=[pltpu.VMEM((D_GROUP // LANES, nseq * T, LANES), F32),
                        pltpu.VMEM((nseq * T, D_GROUP), F32),
                        pltpu.VMEM((nseq * T, ns_all), F32), pltpu.VMEM((nseq * T, ns_all), F32),
                        pltpu.VMEM((nseq, ns_all), F32), pltpu.VMEM((nseq, ns_all), F32)],
        compiler_params=_params(2),
        name="s5",
    )(u3, u3, bd, cd, a_re, a_im, dskip, wglu, bglu, s0_re, s0_im)
    return y.reshape(batch * seq, D_GROUP), s_re, s_im


def _rel_bias_table(rel_bias, n_q, n_k, offset, band):
    length = n_k + n_q - 1
    j = jnp.arange(length)
    idx = jnp.clip(offset + n_q - 1 - j, -REL_CLIP, REL_CLIP) + REL_CLIP
    ext = jnp.pad(rel_bias[..., idx].astype(F32), [(0, 0)] * (rel_bias.ndim - 1) + [(0, 1)])
    lead = ext.shape[:-1]
    flat = jnp.tile(ext, n_q)[..., :n_q * length]
    tbl = flat.reshape(lead + (n_q, length))[..., n_q - 1:n_q - 1 + n_k]
    if band:
        qc = jnp.arange(n_q)[:, None] // CHUNK
        kc = jnp.arange(n_k)[None, :] // CHUNK
        tbl = jnp.where((kc >= qc) & (kc <= qc + BAND), tbl, -jnp.inf)
    return tbl


def _s5_params(a_re, a_im, log_dt, b_re, b_im, c_re, c_im):
    dt = jnp.exp(log_dt)[:, None]
    mag = jnp.exp(a_re * dt)
    ab_re, ab_im = mag * jnp.cos(a_im * dt), mag * jnp.sin(a_im * dt)
    den = a_re * a_re + a_im * a_im
    co_re = ((ab_re - 1.0) * a_re + ab_im * a_im) / den
    co_im = (ab_im * a_re - (ab_re - 1.0) * a_im) / den
    bb_re = co_re[..., None] * b_re - co_im[..., None] * b_im
    bb_im = co_re[..., None] * b_im + co_im[..., None] * b_re
    a8_re = jnp.broadcast_to(ab_re.reshape(1, -1), (SUBLANES, S5_GROUPS * S5_STATE))
    a8_im = jnp.broadcast_to(ab_im.reshape(1, -1), (SUBLANES, S5_GROUPS * S5_STATE))
    gpb = S5_GROUPS // S5_BLOCKS
    eye = jnp.eye(gpb, dtype=F32)

    def blockdiag_in(bb):
        bbk = bb.reshape(S5_BLOCKS, gpb, S5_STATE, S5_CH)
        return jnp.einsum('kgpc,gh->kgchp', bbk, eye).reshape(S5_BLOCKS, gpb * S5_CH, gpb * S5_STATE)

    def blockdiag_out(cc):
        cck = cc.reshape(S5_BLOCKS, gpb, S5_CH, S5_STATE)
        return jnp.einsum('kgcp,gh->kgphc', cck, eye).reshape(S5_BLOCKS, gpb * S5_STATE, gpb * S5_CH)

    bd = jnp.concatenate([blockdiag_in(bb_re), blockdiag_in(bb_im)], axis=-1).astype(BF16)
    cd = jnp.concatenate([blockdiag_out(c_re), blockdiag_out(-c_im)], axis=1).astype(BF16)
    return bd, cd, a8_re, a8_im


def _tiles(batch, seq):
    return (min(batch * seq, IN_PROJ_ROWS), IN_PROJ_COLS, min(seq, OUT_PROJ_ROWS),
            min(seq, MLSTM_CHUNK), min(seq, S5_FRAMES))


def _layer(x, xn, batch, seq, att_cache, ml_state, conv_state, s5_state, p, layer, last, tiles):
    tm_in, tn_in, tm_out, L, t_s5 = tiles
    u, gates = _inproj(xn, p["w_in_t_all"], layer, tm_in, tn_in)
    nc = seq // L
    gates_t = gates[:, :SUBLANES].reshape(batch, nc, L, SUBLANES).transpose(0, 1, 3, 2)

    if att_cache is None:
        ya = _attn_prompt(u, p["bias"], batch, seq)
        rows = min(BAND * CHUNK, seq)
        u3 = u.reshape(batch, seq, N_MAIN)
        new_k = u3[:, seq - rows:, COL_AK * D_GROUP:(COL_AK + 1) * D_GROUP].astype(F32)
        new_v = u3[:, seq - rows:, COL_AV * D_GROUP:(COL_AV + 1) * D_GROUP].astype(F32)
    else:
        ya, new_k, new_v = _attn_sample(u, att_cache[0], att_cache[1], layer, p["bias"], batch, seq)
    new_k = new_k.reshape(batch, -1, ATT_HEADS, ATT_DH)
    new_v = new_v.reshape(batch, -1, ATT_HEADS, ATT_DH)

    c0, n0, m0 = ml_state
    m0 = jnp.broadcast_to(jnp.pad(m0, ((0, 0), (0, SUBLANES - ML_HEADS)))[:, :, None],
                          (batch, SUBLANES, ML_DH))
    ym, c_new, n_new, m_new = _mlstm(u, gates, gates_t, p["b_col"], p["b_row"], c0, n0, m0, batch, seq, L)
    m_new = m_new[:, :ML_HEADS, 0]

    ys, s5_re, s5_im = _s5(u, p["bd"], p["cd"], p["a_re"], p["a_im"], p["dskip"], p["wglu"], p["bglu"],
                           s5_state[0].reshape(batch, -1), s5_state[1].reshape(batch, -1),
                           batch, seq, t_s5)
    s5_re = s5_re.reshape(batch, S5_GROUPS, S5_STATE)
    s5_im = s5_im.reshape(batch, S5_GROUPS, S5_STATE)

    *outs, conv_new = _outproj(ya, ym, ys, u, p["conv_w"], conv_state, x, p["w_out_all"], layer, p["g_next"],
                               last, batch, seq, tm_out)
    return outs, (new_k, new_v, c_new, n_new, m_new, conv_new, s5_re, s5_im)


def kernel(x_prompt, x_sample, cache_attn_k, cache_attn_v, state_mlstm_C, state_mlstm_n, state_mlstm_m, state_conv, state_s5_re, state_s5_im, norm_g, w_in, w_out, attn_rel_bias, mlstm_b_if, conv_w, s5_A_re, s5_A_im, s5_log_dt, s5_B_re, s5_B_im, s5_C_re, s5_C_im, s5_D, s5_w_glu, s5_b_glu, final_norm_g):
    depth = w_in.shape[0]
    bp, sp, _ = x_prompt.shape
    bs, ss, _ = x_sample.shape
    w_rows = cache_attn_k.shape[2]
    gate_lo = 9 * D_GROUP
    gate_hi = gate_lo + 2 * ML_HEADS

    xp = x_prompt.reshape(bp * sp, D_MODEL)
    xs = x_sample.reshape(bs * ss, D_MODEL)
    ml0 = (jnp.zeros((bp, ML_HEADS, ML_DH, ML_DH), F32), jnp.zeros((bp, ML_HEADS, ML_DH), F32),
           jnp.zeros((bp, ML_HEADS), F32))
    conv0 = jnp.zeros((bp, CONV_W - 1, D_GROUP), F32)
    s50 = (jnp.zeros((bp, S5_GROUPS, S5_STATE), F32), jnp.zeros((bp, S5_GROUPS, S5_STATE), F32))

    tiles_p = _tiles(bp, sp)
    tiles_s = _tiles(bs, ss)
    att_rows = ATT_CHUNKS_PER_STEP * CHUNK
    bias_p = _rel_bias_table(attn_rel_bias, att_rows, BAND * CHUNK + att_rows, BAND * CHUNK, True)
    bias_s = _rel_bias_table(attn_rel_bias, ss, w_rows + ss, w_rows, False)
    w_in_t = jnp.swapaxes(w_in, 1, 2)
    w_in_t_all = jnp.concatenate(
        [w_in_t[:, :gate_lo], w_in_t[:, gate_hi:],
         jnp.pad(w_in_t[:, gate_lo:gate_hi], ((0, 0), (0, GATE_PAD - 2 * ML_HEADS), (0, 0)))],
        axis=1).astype(BF16)
    w_out_all = w_out.astype(BF16)
    g_all = jnp.concatenate([norm_g, final_norm_g[None]], axis=0).reshape(depth + 1, 1, D_MODEL)
    xnp = _prenorm(xp, g_all[0], 512)
    xns = _prenorm(xs, g_all[0], bs * ss)

    st_p, st_s = [], []
    for l in range(depth):
        bd, cd, a8_re, a8_im = _s5_params(s5_A_re[l], s5_A_im[l], s5_log_dt[l], s5_B_re[l], s5_B_im[l],
                                          s5_C_re[l], s5_C_im[l])
        b_if = mlstm_b_if[l]
        last = l == depth - 1
        p = {
            "w_in_t_all": w_in_t_all, "w_out_all": w_out_all, "g_next": g_all[l + 1],
            "b_col": jnp.pad(b_if, (0, GATE_PAD - 2 * ML_HEADS)).reshape(1, GATE_PAD),
            "b_row": b_if.reshape(SUBLANES, 1),
            "conv_w": conv_w[l],
            "bd": bd, "cd": cd, "a_re": a8_re, "a_im": a8_im,
            "dskip": s5_D[l].reshape(1, D_GROUP),
            "wglu": s5_w_glu[l].astype(BF16),
            "bglu": s5_b_glu[l].reshape(1, D_GROUP),
        }
        p_prompt = dict(p, bias=bias_p[l])
        p_sample = dict(p, bias=bias_s[l])

        out_p, sp_l = _layer(xp, xnp, bp, sp, None, ml0, conv0, s50, p_prompt, l, last, tiles_p)
        out_s, ss_l = _layer(xs, xns, bs, ss,
                             (cache_attn_k.reshape(depth, bs, w_rows, D_GROUP),
                              cache_attn_v.reshape(depth, bs, w_rows, D_GROUP)),
                             (state_mlstm_C[l], state_mlstm_n[l], state_mlstm_m[l]),
                             state_conv[l], (state_s5_re[l], state_s5_im[l]), p_sample, l, last, tiles_s)
        st_p.append(sp_l)
        st_s.append(ss_l)
        if not last:
            (xp, xnp), (xs, xns) = out_p, out_s

    y_prompt = out_p[0].reshape(bp, sp, D_MODEL)
    y_sample = out_s[0].reshape(bs, ss, D_MODEL)
    outs_p = [jnp.stack(t) for t in zip(*st_p)]
    outs_s = [jnp.stack(t) for t in zip(*st_s)]
    return (y_prompt, y_sample, *outs_p, *outs_s)
```

```python
import functools
import math

import jax
import jax.numpy as jnp
from jax import lax
from jax.experimental import pallas as pl
from jax.experimental.pallas import tpu as pltpu

F32 = jnp.float32
BF16 = jnp.bfloat16

D_MODEL = 2048
D_GROUP = 512
CHUNK = 64
BAND = 8
REL_CLIP = 128
ATT_HEADS = 8
ATT_DH = 64
ML_HEADS = 4
ML_DH = 128
CONV_W = 3
S5_GROUPS = 32
S5_CH = 16
S5_STATE = 64
S5_BLOCKS = 4
S5_BLK_STATES = S5_GROUPS * S5_STATE // S5_BLOCKS
NORM_EPS = 1e-6
N_MAIN = 15 * D_GROUP
N_HEAD = 9 * D_GROUP
GATE_PAD = 128
SUBLANES = 8
LANES = 128
IN_PROJ_ROWS = 1024
IN_PROJ_COLS = 3 * D_GROUP
OUT_PROJ_ROWS = 512
MLSTM_CHUNK = 4 * CHUNK
S5_FRAMES = 128
ATT_CHUNKS_PER_STEP = 4
VMEM_LIMIT = 56 * 1024 * 1024

COL_AQ, COL_AK, COL_AV, COL_AZ = 0, 1, 2, 3
COL_MQ, COL_MK, COL_MV, COL_MO, COL_MZ = 4, 5, 6, 7, 8
COL_CB, COL_CC, COL_CX, COL_CZ = 9, 10, 11, 12
COL_SU, COL_SZ = 13, 14


def _params(n_axes):
    return pltpu.CompilerParams(dimension_semantics=("arbitrary",) * n_axes,
                                vmem_limit_bytes=VMEM_LIMIT)


def _sigmoid(x):
    return 1.0 / (1.0 + jnp.exp(-x))


def _silu(x):
    return x * _sigmoid(x)


def _log_sigmoid(x):
    return jnp.minimum(x, 0.0) - jnp.log(1.0 + jnp.exp(-jnp.abs(x)))


def _gelu_tanh(x):
    return 0.5 * x * (1.0 + jnp.tanh(math.sqrt(2.0 / math.pi) * (x + 0.044715 * (x * x * x))))


def _rms(x, g):
    return x * lax.rsqrt(jnp.mean(x * x, axis=-1, keepdims=True) + NORM_EPS) * g


def _dot(a, b):
    return jnp.dot(a, b, preferred_element_type=F32)


def _dot_nt(a, b):
    return lax.dot_general(a, b, (((1,), (1,)), ((), ())), preferred_element_type=F32)


def _dot_exact(a, b):
    return jnp.dot(a, b, preferred_element_type=F32, precision=lax.Precision.HIGHEST)


def _prenorm_kernel(x_ref, g_ref, o_ref):
    o_ref[...] = _rms(x_ref[...], g_ref[...]).astype(o_ref.dtype)


def _prenorm(x, g, tm):
    n = x.shape[0]
    return pl.pallas_call(
        _prenorm_kernel,
        grid=(n // tm,),
        in_specs=[pl.BlockSpec((tm, D_MODEL), lambda i: (i, 0)),
                  pl.BlockSpec((1, D_MODEL), lambda i: (0, 0))],
        out_specs=pl.BlockSpec((tm, D_MODEL), lambda i: (i, 0)),
        out_shape=jax.ShapeDtypeStruct((n, D_MODEL), BF16),
        compiler_params=_params(1),
        name="prenorm",
    )(x, g)


def _inproj_kernel(xn_ref, wh_ref, wt_ref, wg_ref, u_ref, gate_ref, *, head_tiles):
    j = pl.program_id(1)

    @pl.when(j == 0)
    def _():
        gate_ref[...] = _dot_nt(xn_ref[...], wg_ref[...])

    @pl.when(j < head_tiles)
    def _():
        u_ref[...] = _dot_nt(xn_ref[...], wh_ref[...]).astype(u_ref.dtype)

    @pl.when(j >= head_tiles)
    def _():
        u_ref[...] = _dot_nt(xn_ref[...], wt_ref[...]).astype(u_ref.dtype)


def _inproj(xn, w_head, w_tail, w_gate, layer, tm, tn):
    n = xn.shape[0]
    head_tiles = N_HEAD // tn
    return pl.pallas_call(
        functools.partial(_inproj_kernel, head_tiles=head_tiles),
        grid=(n // tm, N_MAIN // tn),
        in_specs=[
            pl.BlockSpec((tm, D_MODEL), lambda i, j: (i, 0)),
            pl.BlockSpec((None, tn, D_MODEL), lambda i, j: (layer, jnp.minimum(j, head_tiles - 1), 0)),
            pl.BlockSpec((None, tn, D_MODEL), lambda i, j: (layer, jnp.maximum(j - head_tiles, 0), 0)),
            pl.BlockSpec((None, GATE_PAD, D_MODEL), lambda i, j: (layer, 0, 0)),
        ],
        out_specs=[
            pl.BlockSpec((tm, tn), lambda i, j: (i, j)),
            pl.BlockSpec((tm, GATE_PAD), lambda i, j: (i, 0)),
        ],
        out_shape=[
            jax.ShapeDtypeStruct((n, N_MAIN), BF16),
            jax.ShapeDtypeStruct((n, GATE_PAD), F32),
        ],
        compiler_params=_params(2),
        name="inproj",
    )(xn, w_head, w_tail, w_gate)


def _outproj_kernel(ya_ref, ym_ref, ys_ref, cb_ref, cc_ref, cx_ref, cz_ref, cw_ref, cs0_ref,
                    x_ref, w_ref, g_ref, *refs, tiles_per_seq):
    *out_refs, cs_ref, u_s = refs
    tm = x_ref.shape[0]
    head = SUBLANES
    taps = CONV_W - 1

    @pl.when(pl.program_id(0) % tiles_per_seq == 0)
    def _():
        u_s[head - taps:head, :] = cs0_ref[0]

    u_s[head:, :] = cc_ref[...].astype(F32) * cx_ref[...].astype(F32)
    y = cw_ref[0:1, :] * u_s[head - 2:head - 2 + tm, :]
    y = y + cw_ref[1:2, :] * u_s[head - 1:head - 1 + tm, :]
    y = y + cw_ref[2:3, :] * u_s[head:, :]
    yc = (_silu(cz_ref[...].astype(F32)) * (cb_ref[...].astype(F32) * y)).astype(BF16)
    last_rows = u_s[head + tm - taps:, :]
    cs_ref[0] = last_rows
    u_s[head - taps:head, :] = last_rows

    acc = x_ref[...] + _dot(jnp.concatenate([ya_ref[...], ym_ref[...], yc, ys_ref[...]], axis=1), w_ref[...])
    normed = _rms(acc, g_ref[...])
    if len(out_refs) == 2:
        out_refs[0][...] = acc
    out_refs[-1][...] = normed.astype(out_refs[-1].dtype)


def _outproj(ya, ym, ys, u, conv_w, conv_s0, x, w_out_all, layer, g_next, last, batch, seq, tm):
    n = x.shape[0]
    tps = seq // tm
    yspec = pl.BlockSpec((tm, D_GROUP), lambda i: (i, 0))
    row = pl.BlockSpec((tm, D_MODEL), lambda i: (i, 0))
    st = pl.BlockSpec((1, CONV_W - 1, D_GROUP), lambda i: (i // tps, 0, 0))

    def tok(colblk):
        return pl.BlockSpec((tm, D_GROUP), lambda i: (i, colblk))

    out_specs, out_shape = [row], [jax.ShapeDtypeStruct((n, D_MODEL), F32)]
    if not last:
        out_specs.append(row)
        out_shape.append(jax.ShapeDtypeStruct((n, D_MODEL), BF16))
    out_specs.append(st)
    out_shape.append(jax.ShapeDtypeStruct((batch, CONV_W - 1, D_GROUP), F32))
    return pl.pallas_call(
        functools.partial(_outproj_kernel, tiles_per_seq=tps),
        grid=(n // tm,),
        in_specs=[yspec, yspec, yspec, tok(COL_CB), tok(COL_CC), tok(COL_CX), tok(COL_CZ),
                  pl.BlockSpec((CONV_W, D_GROUP), lambda i: (0, 0)), st, row,
                  pl.BlockSpec((None, D_MODEL, D_MODEL), lambda i: (layer, 0, 0)),
                  pl.BlockSpec((1, D_MODEL), lambda i: (0, 0))],
        out_specs=out_specs,
        out_shape=out_shape,
        scratch_shapes=[pltpu.VMEM((SUBLANES + tm, D_GROUP), F32)],
        compiler_params=_params(1),
        name="outproj",
    )(ya, ym, ys, u, u, u, u, conv_w, conv_s0, x, w_out_all, g_next)


def _attn_heads(q, z, k_all, v_all, bias_ref, first_key, o_ref):
    lq, nk = q.shape[0], k_all.shape[0]
    pair_w = 2 * ATT_DH
    first = lax.broadcasted_iota(jnp.int32, (lq, pair_w), 1) < ATT_DH
    kcol = lax.broadcasted_iota(jnp.int32, (2 * lq, nk), 1)
    ones = jnp.ones((nk, pair_w), BF16)
    for pair in range(ATT_HEADS // 2):
        sl = slice(pair * pair_w, (pair + 1) * pair_w)
        qp = q[:, sl] * (ATT_DH ** -0.5)
        kp, vp = k_all[:, sl], v_all[:, sl]
        q2 = jnp.concatenate([jnp.where(first, qp, 0.0), jnp.where(first, 0.0, qp)], axis=0).astype(BF16)
        s = _dot_nt(q2, kp) + jnp.concatenate([bias_ref[2 * pair], bias_ref[2 * pair + 1]], axis=0)
        if first_key is not None:
            s = jnp.where(kcol >= first_key, s, -jnp.inf)
        p = jnp.exp(s - jnp.max(s, axis=-1, keepdims=True)).astype(BF16)
        ol = _dot(p, jnp.concatenate([vp, ones], axis=1))
        on = ol[:, :pair_w] * (1.0 / ol[:, pair_w:])
        o = jnp.where(first, on[:lq], on[lq:])
        o_ref[:, sl] = (_silu(z[:, sl]) * o).astype(o_ref.dtype)


def _attn_prompt_kernel(q_ref, k_ref, v_ref, z_ref, bias_ref, o_ref, kpad_ref, vpad_ref):
    step = pl.program_id(1)
    pad = BAND * CHUNK
    rows = q_ref.shape[0]
    nk = pad + rows

    @pl.when(step == 0)
    def _():
        kpad_ref[0:pad, :] = jnp.zeros((pad, D_GROUP), BF16)
        vpad_ref[0:pad, :] = jnp.zeros((pad, D_GROUP), BF16)
        kpad_ref[pad:, :] = k_ref[...].astype(BF16)
        vpad_ref[pad:, :] = v_ref[...].astype(BF16)

    start = pl.multiple_of(step * rows, rows)
    k_all = kpad_ref[pl.ds(start, nk), :]
    v_all = vpad_ref[pl.ds(start, nk), :]
    _attn_heads(q_ref[...].astype(F32), z_ref[...].astype(F32), k_all, v_all, bias_ref,
                pad - step * rows, o_ref)


def _attn_prompt(u, bias, batch, seq):
    rows, nk = bias.shape[1], bias.shape[2]
    ns = seq // rows
    return pl.pallas_call(
        _attn_prompt_kernel,
        grid=(batch, ns),
        in_specs=[
            pl.BlockSpec((rows, D_GROUP), lambda b, s: (b * ns + s, COL_AQ)),
            pl.BlockSpec((seq, D_GROUP), lambda b, s: (b, COL_AK)),
            pl.BlockSpec((seq, D_GROUP), lambda b, s: (b, COL_AV)),
            pl.BlockSpec((rows, D_GROUP), lambda b, s: (b * ns + s, COL_AZ)),
            pl.BlockSpec((ATT_HEADS, rows, nk), lambda b, s: (0, 0, 0)),
        ],
        out_specs=pl.BlockSpec((rows, D_GROUP), lambda b, s: (b * ns + s, 0)),
        out_shape=jax.ShapeDtypeStruct((batch * seq, D_GROUP), BF16),
        scratch_shapes=[pltpu.VMEM((seq + BAND * CHUNK, D_GROUP), BF16),
                        pltpu.VMEM((seq + BAND * CHUNK, D_GROUP), BF16)],
        compiler_params=_params(2),
        name="attn_prompt",
    )(u, u, u, u, bias)


def _attn_sample_kernel(q_ref, k_ref, v_ref, z_ref, kc_ref, vc_ref, bias_ref,
                        o_ref, kn_ref, vn_ref, kall_ref, vall_ref):
    w = kc_ref.shape[1]
    t = q_ref.shape[0]
    kall_ref[0:w, :] = kc_ref[0].astype(BF16)
    vall_ref[0:w, :] = vc_ref[0].astype(BF16)
    kall_ref[w:, :] = k_ref[...].astype(BF16)
    vall_ref[w:, :] = v_ref[...].astype(BF16)
    kn_ref[0, 0:w - t, :] = kc_ref[0, t:, :]
    vn_ref[0, 0:w - t, :] = vc_ref[0, t:, :]
    kn_ref[0, w - t:, :] = k_ref[...].astype(F32)
    vn_ref[0, w - t:, :] = v_ref[...].astype(F32)
    _attn_heads(q_ref[...].astype(F32), z_ref[...].astype(F32), kall_ref[...], vall_ref[...], bias_ref, None, o_ref)


def _attn_sample(u, k_cache_all, v_cache_all, layer, bias, batch, seq):
    w = k_cache_all.shape[2]
    tok = pl.BlockSpec((seq, D_GROUP), lambda b: (b, 0))
    cache = pl.BlockSpec((1, w, D_GROUP), lambda b: (b, 0, 0))
    cache_in = pl.BlockSpec((None, 1, w, D_GROUP), lambda b: (layer, b, 0, 0))
    return pl.pallas_call(
        _attn_sample_kernel,
        grid=(batch,),
        in_specs=[
            pl.BlockSpec((seq, D_GROUP), lambda b: (b, COL_AQ)),
            pl.BlockSpec((seq, D_GROUP), lambda b: (b, COL_AK)),
            pl.BlockSpec((seq, D_GROUP), lambda b: (b, COL_AV)),
            pl.BlockSpec((seq, D_GROUP), lambda b: (b, COL_AZ)),
            cache_in, cache_in,
            pl.BlockSpec((ATT_HEADS, seq, w + seq), lambda b: (0, 0, 0)),
        ],
        out_specs=[tok, cache, cache],
        out_shape=[jax.ShapeDtypeStruct((batch * seq, D_GROUP), BF16),
                   jax.ShapeDtypeStruct((batch, w, D_GROUP), F32),
                   jax.ShapeDtypeStruct((batch, w, D_GROUP), F32)],
        scratch_shapes=[pltpu.VMEM((w + seq, D_GROUP), BF16),
                        pltpu.VMEM((w + seq, D_GROUP), BF16)],
        compiler_params=_params(1),
        name="attn_sample",
    )(u, u, u, u, k_cache_all, v_cache_all, bias)


def _mlstm_kernel(q_ref, k_ref, v_ref, o_ref, z_ref, gc_ref, gr_ref, bc_ref, br_ref,
                  c0_ref, n0_ref, m0_ref,
                  y_ref, cout_ref, nout_ref, mout_ref, c_s, n_s, m_s):
    c = pl.program_id(1)
    L = q_ref.shape[0]

    @pl.when(c == 0)
    def _():
        c_s[...] = c0_ref[0]
        n_s[...] = n0_ref[0]
        m_s[...] = m0_ref[0]

    row = lax.broadcasted_iota(jnp.int32, (L, L), 0)
    col = lax.broadcasted_iota(jnp.int32, (L, L), 1)
    tril = (row >= col).astype(F32)
    triu = (row <= col).astype(F32)
    causal = row >= col
    ones_lv = jnp.ones((L, ML_DH), BF16)
    zeros_cn = jnp.zeros((ML_DH - 1, ML_DH), F32)
    eye_bf = (lax.broadcasted_iota(jnp.int32, (ML_DH, ML_DH), 0)
              == lax.broadcasted_iota(jnp.int32, (ML_DH, ML_DH), 1)).astype(BF16)

    g_col = gc_ref[...] + bc_ref[...]
    g_row = gr_ref[0, 0] + br_ref[...]
    b_col = _dot_exact(tril, _log_sigmoid(g_col))
    b_row = _dot_exact(_log_sigmoid(g_row), triu)

    m_all, n_all = m_s[...], n_s[...]
    c_all = [c_s[h] for h in range(ML_HEADS)]
    c_new, n_new, m_new_rows = [], [], []

    for h in range(ML_HEADS):
        sl = slice(h * ML_DH, (h + 1) * ML_DH)
        bt = b_col[:, ML_HEADS + h:ML_HEADS + h + 1]
        ig_t = g_col[:, h:h + 1]
        bs = b_row[ML_HEADS + h:ML_HEADS + h + 1, :]
        ig_s = g_row[h:h + 1, :]
        m_prev = m_all[h:h + 1, 0:1]
        n_prev = n_all[h:h + 1, :]
        c_prev = c_all[h]

        qh = q_ref[:, sl].astype(F32)
        kh = k_ref[:, sl].astype(F32) * (ML_DH ** -0.5)
        vh = v_ref[:, sl].astype(F32)
        qb, kb = qh.astype(BF16), kh.astype(BF16)

        d = jnp.where(causal, bt - bs + ig_s, -jnp.inf)
        m_inter = bt + m_prev
        m_t = jnp.maximum(m_inter, jnp.max(d, axis=-1, keepdims=True))
        w_intra = jnp.exp(d - m_t) * _dot_nt(qb, kb)
        w_inter = jnp.exp(m_inter - m_t)
        pv = _dot(w_intra.astype(BF16), jnp.concatenate([vh.astype(BF16), ones_lv], axis=1))
        cq = _dot_nt(qb, jnp.concatenate([c_prev, n_prev, zeros_cn], axis=0).astype(BF16))
        num = w_inter * cq[:, :ML_DH] + pv[:, :ML_DH]
        den = w_inter * cq[:, ML_DH:ML_DH + 1] + pv[:, ML_DH:ML_DH + 1]
        hh = num / jnp.maximum(jnp.abs(den), jnp.exp(-m_t))
        ym = _sigmoid(o_ref[:, sl].astype(F32)) * hh
        y_ref[:, sl] = (_silu(z_ref[:, sl].astype(F32)) * ym).astype(y_ref.dtype)

        b_last = bt[L - 1:L, :]
        m_new = jnp.maximum(b_last + m_prev, jnp.max(b_last - bs + ig_s, axis=-1, keepdims=True))
        ws = jnp.exp(b_last - bt + ig_t - m_new)
        decay = jnp.exp(b_last + m_prev - m_new)
        vw_t = _dot_nt(eye_bf, (vh * ws).astype(BF16)).astype(BF16)
        c_new.append(decay * c_prev + _dot(vw_t, kb))
        n_new.append(decay * n_prev + jnp.sum(ws * kh, axis=0, keepdims=True))
        m_new_rows.append(jnp.broadcast_to(m_new, (1, m_s.shape[1])))

    for h in range(ML_HEADS):
        c_s[h] = c_new[h]
        cout_ref[0, h] = c_new[h]
    n_cat = jnp.concatenate(n_new, axis=0)
    n_s[...] = n_cat
    nout_ref[0] = n_cat
    m_cat = jnp.concatenate(m_new_rows + [m_all[ML_HEADS:, :]], axis=0)
    m_s[...] = m_cat
    mout_ref[0] = m_cat


def _mlstm(u, gates, gates_t, b_col, b_row, c0, n0, m0, batch, seq, L):
    nc = seq // L

    def tok(colblk):
        return pl.BlockSpec((L, D_GROUP), lambda b, c: (b * nc + c, colblk))

    st_c = pl.BlockSpec((1, ML_HEADS, ML_DH, ML_DH), lambda b, c: (b, 0, 0, 0))
    st_n = pl.BlockSpec((1, ML_HEADS, ML_DH), lambda b, c: (b, 0, 0))
    st_m = pl.BlockSpec((1, SUBLANES, ML_DH), lambda b, c: (b, 0, 0))
    return pl.pallas_call(
        _mlstm_kernel,
        grid=(batch, nc),
        in_specs=[
            tok(COL_MQ), tok(COL_MK), tok(COL_MV), tok(COL_MO), tok(COL_MZ),
            pl.BlockSpec((L, GATE_PAD), lambda b, c: (b * nc + c, 0)),
            pl.BlockSpec((1, 1, SUBLANES, L), lambda b, c: (b, c, 0, 0)),
            pl.BlockSpec((1, GATE_PAD), lambda b, c: (0, 0)),
            pl.BlockSpec((SUBLANES, 1), lambda b, c: (0, 0)),
            st_c, st_n, st_m,
        ],
        out_specs=[pl.BlockSpec((L, D_GROUP), lambda b, c: (b * nc + c, 0)), st_c, st_n, st_m],
        out_shape=[jax.ShapeDtypeStruct((batch * seq, D_GROUP), BF16),
                   jax.ShapeDtypeStruct((batch, ML_HEADS, ML_DH, ML_DH), F32),
                   jax.ShapeDtypeStruct((batch, ML_HEADS, ML_DH), F32),
                   jax.ShapeDtypeStruct((batch, SUBLANES, ML_DH), F32)],
        scratch_shapes=[pltpu.VMEM((ML_HEADS, ML_DH, ML_DH), F32),
                        pltpu.VMEM((ML_HEADS, ML_DH), F32),
                        pltpu.VMEM((SUBLANES, ML_DH), F32)],
        compiler_params=_params(2),
        name="mlstm",
    )(u, u, u, u, u, gates, gates_t, b_col, b_row, c0, n0, m0)


def _s5_kernel(u_ref, z_ref, bd_ref, cd_ref, are_ref, aim_ref, dskip_ref, wglu_ref, bglu_ref,
               s0re_ref, s0im_ref, y_ref, sre_ref, sim_ref, perm_s, up_s, xr_s, xi_s, car_s, cai_s):
    t = pl.program_id(1)
    nseq, T = u_ref.shape[0], u_ref.shape[1]
    rows = nseq * T
    NS = S5_BLK_STATES
    ntile = D_GROUP // LANES

    @pl.when(t == 0)
    def _():
        car_s[...] = s0re_ref[...]
        cai_s[...] = s0im_ref[...]

    u_seq = u_ref[...].astype(F32).reshape(rows, D_GROUP)
    for j in range(ntile):
        perm_s[j] = u_seq[:, j * LANES:(j + 1) * LANES]

    def to_frame_major(g, carry):
        dst = pl.ds(pl.multiple_of(g * nseq, nseq), nseq)
        for j in range(ntile):
            up_s[dst, j * LANES:(j + 1) * LANES] = perm_s[j, pl.ds(g, nseq, stride=T), :]
        return carry

    lax.fori_loop(0, T, to_frame_major, 0, unroll=8)
    u = up_s[...]
    ub = u.astype(BF16)
    for blk in range(S5_BLOCKS):
        cols = slice(blk * NS, (blk + 1) * NS)
        bu = _dot(ub[:, blk * 128:(blk + 1) * 128], bd_ref[blk])
        xr_s[:, cols] = bu[:, :NS]
        xi_s[:, cols] = bu[:, NS:]

    for blk in range(S5_BLOCKS):
        cols = slice(blk * NS, (blk + 1) * NS)
        ar = are_ref[:, cols]
        ai = aim_ref[:, cols]

        def body(g, carry, cols=cols, ar=ar, ai=ai):
            xr, xi = carry
            frame = pl.ds(pl.multiple_of(g * nseq, nseq), nseq)
            nr = ar * xr - ai * xi + xr_s[frame, cols]
            ni = ar * xi + ai * xr + xi_s[frame, cols]
            xr_s[frame, cols] = nr
            xi_s[frame, cols] = ni
            return nr, ni

        cr, ci = lax.fori_loop(0, T, body, (car_s[:, cols], cai_s[:, cols]), unroll=4)
        car_s[:, cols] = cr
        cai_s[:, cols] = ci

    ys = []
    for blk in range(S5_BLOCKS):
        cols = slice(blk * NS, (blk + 1) * NS)
        ys.append(_dot(xr_s[:, cols].astype(BF16), cd_ref[blk, :NS, :])
                  + _dot(xi_s[:, cols].astype(BF16), cd_ref[blk, NS:, :]))
    y = _gelu_tanh(jnp.concatenate(ys, axis=1) + dskip_ref[...] * u)
    y = y * _sigmoid(_dot(y.astype(BF16), wglu_ref[...]) + bglu_ref[...])
    for j in range(ntile):
        perm_s[j] = y[:, j * LANES:(j + 1) * LANES]
    for i in range(nseq):
        y_i = jnp.concatenate([perm_s[j, pl.ds(i, T, stride=nseq), :] for j in range(ntile)], axis=1)
        y_ref[i] = (_silu(z_ref[i].astype(F32)) * y_i).astype(y_ref.dtype)
    sre_ref[...] = car_s[...]
    sim_ref[...] = cai_s[...]


def _s5(u, bd, cd, a_re, a_im, dskip, wglu, bglu, s0_re, s0_im, batch, seq, T):
    nseq = SUBLANES
    nt = seq // T
    ns_all = S5_GROUPS * S5_STATE
    u3 = u.reshape(batch, seq, N_MAIN)

    def tok(colblk):
        return pl.BlockSpec((nseq, T, D_GROUP), lambda b, t: (b, t, colblk))

    def full(shape):
        return pl.BlockSpec(shape, lambda b, t: (0,) * len(shape))

    st = pl.BlockSpec((nseq, ns_all), lambda b, t: (b, 0))
    y, s_re, s_im = pl.pallas_call(
        _s5_kernel,
        grid=(batch // nseq, nt),
        in_specs=[tok(COL_SU), tok(COL_SZ),
                  full(bd.shape), full(cd.shape), full(a_re.shape), full(a_im.shape),
                  full(dskip.shape), full(wglu.shape), full(bglu.shape), st, st],
        out_specs=[pl.BlockSpec((nseq, T, D_GROUP), lambda b, t: (b, t, 0)), st, st],
        out_shape=[jax.ShapeDtypeStruct((batch, seq, D_GROUP), BF16),
                   jax.ShapeDtypeStruct((batch, ns_all), F32),
                   jax.ShapeDtypeStruct((batch, ns_all), F32)],
        scratch_shapes=[pltpu.VMEM((D_GROUP // LANES, nseq * T, LANES), F32),
                        pltpu.VMEM((nseq * T, D_GROUP), F32),
                        pltpu.VMEM((nseq * T, ns_all), F32), pltpu.VMEM((nseq * T, ns_all), F32),
                        pltpu.VMEM((nseq, ns_all), F32), pltpu.VMEM((nseq, ns_all), F32)],
        compiler_params=_params(2),
        name="s5",
    )(u3, u3, bd, cd, a_re, a_im, dskip, wglu, bglu, s0_re, s0_im)
    return y.reshape(batch * seq, D_GROUP), s_re, s_im


def _rel_bias_table(rel_bias, n_q, n_k, offset, band):
    length = n_k + n_q - 1
    j = jnp.arange(length)
    idx = jnp.clip(offset + n_q - 1 - j, -REL_CLIP, REL_CLIP) + REL_CLIP
    ext = jnp.pad(rel_bias[..., idx].astype(F32), [(0, 0)] * (rel_bias.ndim - 1) + [(0, 1)])
    lead = ext.shape[:-1]
    flat = jnp.tile(ext, n_q)[..., :n_q * length]
    tbl = flat.reshape(lead + (n_q, length))[..., n_q - 1:n_q - 1 + n_k]
    if band:
        qc = jnp.arange(n_q)[:, None] // CHUNK
        kc = jnp.arange(n_k)[None, :] // CHUNK
        tbl = jnp.where((kc >= qc) & (kc <= qc + BAND), tbl, -jnp.inf)
    return tbl


def _s5_params(a_re, a_im, log_dt, b_re, b_im, c_re, c_im):
    dt = jnp.exp(log_dt)[:, None]
    mag = jnp.exp(a_re * dt)
    ab_re, ab_im = mag * jnp.cos(a_im * dt), mag * jnp.sin(a_im * dt)
    den = a_re * a_re + a_im * a_im
    co_re = ((ab_re - 1.0) * a_re + ab_im * a_im) / den
    co_im = (ab_im * a_re - (ab_re - 1.0) * a_im) / den
    bb_re = co_re[..., None] * b_re - co_im[..., None] * b_im
    bb_im = co_re[..., None] * b_im + co_im[..., None] * b_re
    a8_re = jnp.broadcast_to(ab_re.reshape(1, -1), (SUBLANES, S5_GROUPS * S5_STATE))
    a8_im = jnp.broadcast_to(ab_im.reshape(1, -1), (SUBLANES, S5_GROUPS * S5_STATE))
    gpb = S5_GROUPS // S5_BLOCKS
    eye = jnp.eye(gpb, dtype=F32)

    def blockdiag_in(bb):
        bbk = bb.reshape(S5_BLOCKS, gpb, S5_STATE, S5_CH)
        return jnp.einsum('kgpc,gh->kgchp', bbk, eye).reshape(S5_BLOCKS, gpb * S5_CH, gpb * S5_STATE)

    def blockdiag_out(cc):
        cck = cc.reshape(S5_BLOCKS, gpb, S5_CH, S5_STATE)
        return jnp.einsum('kgcp,gh->kgphc', cck, eye).reshape(S5_BLOCKS, gpb * S5_STATE, gpb * S5_CH)

    bd = jnp.concatenate([blockdiag_in(bb_re), blockdiag_in(bb_im)], axis=-1).astype(BF16)
    cd = jnp.concatenate([blockdiag_out(c_re), blockdiag_out(-c_im)], axis=1).astype(BF16)
    return bd, cd, a8_re, a8_im


def _tiles(batch, seq):
    return (min(batch * seq, IN_PROJ_ROWS), IN_PROJ_COLS, min(seq, OUT_PROJ_ROWS),
            min(seq, MLSTM_CHUNK), min(seq, S5_FRAMES))


def _layer(x, xn, batch, seq, att_cache, ml_state, conv_state, s5_state, p, layer, last, tiles):
    tm_in, tn_in, tm_out, L, t_s5 = tiles
    u, gates = _inproj(xn, p["w_head"], p["w_tail"], p["w_gate"], layer, tm_in, tn_in)
    nc = seq // L
    gates_t = gates[:, :SUBLANES].reshape(batch, nc, L, SUBLANES).transpose(0, 1, 3, 2)

    if att_cache is None:
        ya = _attn_prompt(u, p["bias"], batch, seq)
        rows = min(BAND * CHUNK, seq)
        u3 = u.reshape(batch, seq, N_MAIN)
        new_k = u3[:, seq - rows:, COL_AK * D_GROUP:(COL_AK + 1) * D_GROUP].astype(F32)
        new_v = u3[:, seq - rows:, COL_AV * D_GROUP:(COL_AV + 1) * D_GROUP].astype(F32)
    else:
        ya, new_k, new_v = _attn_sample(u, att_cache[0], att_cache[1], layer, p["bias"], batch, seq)
    new_k = new_k.reshape(batch, -1, ATT_HEADS, ATT_DH)
    new_v = new_v.reshape(batch, -1, ATT_HEADS, ATT_DH)

    c0, n0, m0 = ml_state
    m0 = jnp.broadcast_to(jnp.pad(m0, ((0, 0), (0, SUBLANES - ML_HEADS)))[:, :, None],
                          (batch, SUBLANES, ML_DH))
    ym, c_new, n_new, m_new = _mlstm(u, gates, gates_t, p["b_col"], p["b_row"], c0, n0, m0, batch, seq, L)
    m_new = m_new[:, :ML_HEADS, 0]

    ys, s5_re, s5_im = _s5(u, p["bd"], p["cd"], p["a_re"], p["a_im"], p["dskip"], p["wglu"], p["bglu"],
                           s5_state[0].reshape(batch, -1), s5_state[1].reshape(batch, -1),
                           batch, seq, t_s5)
    s5_re = s5_re.reshape(batch, S5_GROUPS, S5_STATE)
    s5_im = s5_im.reshape(batch, S5_GROUPS, S5_STATE)

    *outs, conv_new = _outproj(ya, ym, ys, u, p["conv_w"], conv_state, x, p["w_out_all"], layer, p["g_next"],
                               last, batch, seq, tm_out)
    return outs, (new_k, new_v, c_new, n_new, m_new, conv_new, s5_re, s5_im)


def kernel(x_prompt, x_sample, cache_attn_k, cache_attn_v, state_mlstm_C, state_mlstm_n, state_mlstm_m, state_conv, state_s5_re, state_s5_im, norm_g, w_in, w_out, attn_rel_bias, mlstm_b_if, conv_w, s5_A_re, s5_A_im, s5_log_dt, s5_B_re, s5_B_im, s5_C_re, s5_C_im, s5_D, s5_w_glu, s5_b_glu, final_norm_g):
    depth = w_in.shape[0]
    bp, sp, _ = x_prompt.shape
    bs, ss, _ = x_sample.shape
    w_rows = cache_attn_k.shape[2]
    gate_lo = 9 * D_GROUP
    gate_hi = gate_lo + 2 * ML_HEADS

    xp = x_prompt.reshape(bp * sp, D_MODEL)
    xs = x_sample.reshape(bs * ss, D_MODEL)
    ml0 = (jnp.zeros((bp, ML_HEADS, ML_DH, ML_DH), F32), jnp.zeros((bp, ML_HEADS, ML_DH), F32),
           jnp.zeros((bp, ML_HEADS), F32))
    conv0 = jnp.zeros((bp, CONV_W - 1, D_GROUP), F32)
    s50 = (jnp.zeros((bp, S5_GROUPS, S5_STATE), F32), jnp.zeros((bp, S5_GROUPS, S5_STATE), F32))

    tiles_p = _tiles(bp, sp)
    tiles_s = _tiles(bs, ss)
    att_rows = ATT_CHUNKS_PER_STEP * CHUNK
    bias_p = _rel_bias_table(attn_rel_bias, att_rows, BAND * CHUNK + att_rows, BAND * CHUNK, True)
    bias_s = _rel_bias_table(attn_rel_bias, ss, w_rows + ss, w_rows, False)
    w_head = jnp.swapaxes(w_in, 1, 2).astype(BF16)
    w_tail = w_head[:, gate_hi:]
    w_gate = jnp.pad(w_head[:, gate_lo:gate_hi], ((0, 0), (0, GATE_PAD - 2 * ML_HEADS), (0, 0)))
    w_out_all = w_out.astype(BF16)
    g_all = jnp.concatenate([norm_g, final_norm_g[None]], axis=0).reshape(depth + 1, 1, D_MODEL)
    xnp = _prenorm(xp, g_all[0], 512)
    xns = _prenorm(xs, g_all[0], bs * ss)

    st_p, st_s = [], []
    for l in range(depth):
        bd, cd, a8_re, a8_im = _s5_params(s5_A_re[l], s5_A_im[l], s5_log_dt[l], s5_B_re[l], s5_B_im[l],
                                          s5_C_re[l], s5_C_im[l])
        b_if = mlstm_b_if[l]
        last = l == depth - 1
        p = {
            "w_head": w_head, "w_tail": w_tail, "w_gate": w_gate, "w_out_all": w_out_all, "g_next": g_all[l + 1],
            "b_col": jnp.pad(b_if, (0, GATE_PAD - 2 * ML_HEADS)).reshape(1, GATE_PAD),
            "b_row": b_if.reshape(SUBLANES, 1),
            "conv_w": conv_w[l],
            "bd": bd, "cd": cd, "a_re": a8_re, "a_im": a8_im,
            "dskip": s5_D[l].reshape(1, D_GROUP),
            "wglu": s5_w_glu[l].astype(BF16),
            "bglu": s5_b_glu[l].reshape(1, D_GROUP),
        }
        p_prompt = dict(p, bias=bias_p[l])
        p_sample = dict(p, bias=bias_s[l])

        out_p, sp_l = _layer(xp, xnp, bp, sp, None, ml0, conv0, s50, p_prompt, l, last, tiles_p)
        out_s, ss_l = _layer(xs, xns, bs, ss,
                             (cache_attn_k.reshape(depth, bs, w_rows, D_GROUP),
                              cache_attn_v.reshape(depth, bs, w_rows, D_GROUP)),
                             (state_mlstm_C[l], state_mlstm_n[l], state_mlstm_m[l]),
                             state_conv[l], (state_s5_re[l], state_s5_im[l]), p_sample, l, last, tiles_s)
        st_p.append(sp_l)
        st_s.append(ss_l)
        if not last:
            (xp, xnp), (xs, xns) = out_p, out_s

    y_prompt = out_p[0].reshape(bp, sp, D_MODEL)
    y_sample = out_s[0].reshape(bs, ss, D_MODEL)
    outs_p = [jnp.stack(t) for t in zip(*st_p)]
    outs_s = [jnp.stack(t) for t in zip(*st_s)]
    return (y_prompt, y_sample, *outs_p, *outs_s)
```

```python
import functools
import math

import jax
import jax.numpy as jnp
from jax import lax
from jax.experimental import pallas as pl
from jax.experimental.pallas import tpu as pltpu

F32 = jnp.float32
BF16 = jnp.bfloat16

D_MODEL = 2048
D_GROUP = 512
CHUNK = 64
BAND = 8
REL_CLIP = 128
ATT_HEADS = 8
ATT_DH = 64
ML_HEADS = 4
ML_DH = 128
CONV_W = 3
S5_GROUPS = 32
S5_CH = 16
S5_STATE = 64
S5_BLOCKS = 4
S5_BLK_STATES = S5_GROUPS * S5_STATE // S5_BLOCKS
NORM_EPS = 1e-6
N_MAIN = 15 * D_GROUP
N_HEAD = 9 * D_GROUP
GATE_PAD = 128
SUBLANES = 8
LANES = 128
IN_PROJ_ROWS = 1024
IN_PROJ_COLS = 3 * D_GROUP
OUT_PROJ_ROWS = 512
MLSTM_CHUNK = 4 * CHUNK
S5_FRAMES = 128
ATT_CHUNKS_PER_STEP = 4
VMEM_LIMIT = 56 * 1024 * 1024

COL_AQ, COL_AK, COL_AV, COL_AZ = 0, 1, 2, 3
COL_MQ, COL_MK, COL_MV, COL_MO, COL_MZ = 4, 5, 6, 7, 8
COL_CB, COL_CC, COL_CX, COL_CZ = 9, 10, 11, 12
COL_SU, COL_SZ = 13, 14


def _params(n_axes):
    return pltpu.CompilerParams(dimension_semantics=("arbitrary",) * n_axes,
                                vmem_limit_bytes=VMEM_LIMIT)


def _sigmoid(x):
    return 1.0 / (1.0 + jnp.exp(-x))


def _silu(x):
    return x * _sigmoid(x)


def _log_sigmoid(x):
    return jnp.minimum(x, 0.0) - jnp.log(1.0 + jnp.exp(-jnp.abs(x)))


def _gelu_tanh(x):
    return 0.5 * x * (1.0 + jnp.tanh(math.sqrt(2.0 / math.pi) * (x + 0.044715 * (x * x * x))))


def _rms(x, g):
    return x * lax.rsqrt(jnp.mean(x * x, axis=-1, keepdims=True) + NORM_EPS) * g


def _dot(a, b):
    return jnp.dot(a, b, preferred_element_type=F32)


def _dot_nt(a, b):
    return lax.dot_general(a, b, (((1,), (1,)), ((), ())), preferred_element_type=F32)


def _dot_exact(a, b):
    return jnp.dot(a, b, preferred_element_type=F32, precision=lax.Precision.HIGHEST)


def _prenorm_kernel(x_ref, g_ref, o_ref):
    o_ref[...] = _rms(x_ref[...], g_ref[...]).astype(o_ref.dtype)


def _prenorm(x, g, tm):
    n = x.shape[0]
    return pl.pallas_call(
        _prenorm_kernel,
        grid=(n // tm,),
        in_specs=[pl.BlockSpec((tm, D_MODEL), lambda i: (i, 0)),
                  pl.BlockSpec((1, D_MODEL), lambda i: (0, 0))],
        out_specs=pl.BlockSpec((tm, D_MODEL), lambda i: (i, 0)),
        out_shape=jax.ShapeDtypeStruct((n, D_MODEL), BF16),
        compiler_params=_params(1),
        name="prenorm",
    )(x, g)


def _inproj_kernel(xn_ref, wh_ref, wt_ref, wg_ref, u_ref, gate_ref, *, head_tiles):
    j = pl.program_id(1)

    @pl.when(j == 0)
    def _():
        gate_ref[...] = _dot_nt(xn_ref[...], wg_ref[...])

    @pl.when(j < head_tiles)
    def _():
        u_ref[...] = _dot_nt(xn_ref[...], wh_ref[...]).astype(u_ref.dtype)

    @pl.when(j >= head_tiles)
    def _():
        u_ref[...] = _dot_nt(xn_ref[...], wt_ref[...]).astype(u_ref.dtype)


def _inproj(xn, w_head, w_tail, w_gate, layer, tm, tn):
    n = xn.shape[0]
    head_tiles = N_HEAD // tn
    return pl.pallas_call(
        functools.partial(_inproj_kernel, head_tiles=head_tiles),
        grid=(n // tm, N_MAIN // tn),
        in_specs=[
            pl.BlockSpec((tm, D_MODEL), lambda i, j: (i, 0)),
            pl.BlockSpec((None, tn, D_MODEL), lambda i, j: (layer, jnp.minimum(j, head_tiles - 1), 0)),
            pl.BlockSpec((None, tn, D_MODEL), lambda i, j: (layer, jnp.maximum(j - head_tiles, 0), 0)),
            pl.BlockSpec((None, GATE_PAD, D_MODEL), lambda i, j: (layer, 0, 0)),
        ],
        out_specs=[
            pl.BlockSpec((tm, tn), lambda i, j: (i, j)),
            pl.BlockSpec((tm, GATE_PAD), lambda i, j: (i, 0)),
        ],
        out_shape=[
            jax.ShapeDtypeStruct((n, N_MAIN), BF16),
            jax.ShapeDtypeStruct((n, GATE_PAD), F32),
        ],
        compiler_params=_params(2),
        name="inproj",
    )(xn, w_head, w_tail, w_gate)


def _outproj_kernel(ya_ref, ym_ref, ys_ref, cb_ref, cc_ref, cx_ref, cz_ref, cw_ref, cs0_ref,
                    x_ref, w_ref, g_ref, *refs, tiles_per_seq):
    *out_refs, cs_ref, u_s = refs
    tm = x_ref.shape[0]
    head = SUBLANES
    taps = CONV_W - 1

    @pl.when(pl.program_id(0) % tiles_per_seq == 0)
    def _():
        u_s[head - taps:head, :] = cs0_ref[0]

    u_s[head:, :] = cc_ref[...].astype(F32) * cx_ref[...].astype(F32)
    y = cw_ref[0:1, :] * u_s[head - 2:head - 2 + tm, :]
    y = y + cw_ref[1:2, :] * u_s[head - 1:head - 1 + tm, :]
    y = y + cw_ref[2:3, :] * u_s[head:, :]
    yc = (_silu(cz_ref[...].astype(F32)) * (cb_ref[...].astype(F32) * y)).astype(BF16)
    last_rows = u_s[head + tm - taps:, :]
    cs_ref[0] = last_rows
    u_s[head - taps:head, :] = last_rows

    acc = x_ref[...] + _dot(jnp.concatenate([ya_ref[...], ym_ref[...], yc, ys_ref[...]], axis=1), w_ref[...])
    normed = _rms(acc, g_ref[...])
    if len(out_refs) == 2:
        out_refs[0][...] = acc
    out_refs[-1][...] = normed.astype(out_refs[-1].dtype)


def _outproj(ya, ym, ys, u, conv_w, conv_s0, x, w_out_all, layer, g_next, last, batch, seq, tm):
    n = x.shape[0]
    tps = seq // tm
    yspec = pl.BlockSpec((tm, D_GROUP), lambda i: (i, 0))
    row = pl.BlockSpec((tm, D_MODEL), lambda i: (i, 0))
    st = pl.BlockSpec((1, CONV_W - 1, D_GROUP), lambda i: (i // tps, 0, 0))

    def tok(colblk):
        return pl.BlockSpec((tm, D_GROUP), lambda i: (i, colblk))

    out_specs, out_shape = [row], [jax.ShapeDtypeStruct((n, D_MODEL), F32)]
    if not last:
        out_specs.append(row)
        out_shape.append(jax.ShapeDtypeStruct((n, D_MODEL), BF16))
    out_specs.append(st)
    out_shape.append(jax.ShapeDtypeStruct((batch, CONV_W - 1, D_GROUP), F32))
    return pl.pallas_call(
        functools.partial(_outproj_kernel, tiles_per_seq=tps),
        grid=(n // tm,),
        in_specs=[yspec, yspec, yspec, tok(COL_CB), tok(COL_CC), tok(COL_CX), tok(COL_CZ),
                  pl.BlockSpec((CONV_W, D_GROUP), lambda i: (0, 0)), st, row,
                  pl.BlockSpec((None, D_MODEL, D_MODEL), lambda i: (layer, 0, 0)),
                  pl.BlockSpec((1, D_MODEL), lambda i: (0, 0))],
        out_specs=out_specs,
        out_shape=out_shape,
        scratch_shapes=[pltpu.VMEM((SUBLANES + tm, D_GROUP), F32)],
        compiler_params=_params(1),
        name="outproj",
    )(ya, ym, ys, u, u, u, u, conv_w, conv_s0, x, w_out_all, g_next)


def _attn_heads(q, z, k_all, v_all, bias_ref, first_key, o_ref):
    lq, nk = q.shape[0], k_all.shape[0]
    pair_w = 2 * ATT_DH
    first = lax.broadcasted_iota(jnp.int32, (lq, pair_w), 1) < ATT_DH
    kcol = lax.broadcasted_iota(jnp.int32, (2 * lq, nk), 1)
    ones = jnp.ones((nk, pair_w), BF16)
    for pair in range(ATT_HEADS // 2):
        sl = slice(pair * pair_w, (pair + 1) * pair_w)
        qp = q[:, sl] * (ATT_DH ** -0.5)
        kp, vp = k_all[:, sl], v_all[:, sl]
        q2 = jnp.concatenate([jnp.where(first, qp, 0.0), jnp.where(first, 0.0, qp)], axis=0).astype(BF16)
        s = _dot_nt(q2, kp) + jnp.concatenate([bias_ref[2 * pair], bias_ref[2 * pair + 1]], axis=0)
        if first_key is not None:
            s = jnp.where(kcol >= first_key, s, -jnp.inf)
        p = jnp.exp(s - jnp.max(s, axis=-1, keepdims=True)).astype(BF16)
        ol = _dot(p, jnp.concatenate([vp, ones], axis=1))
        on = ol[:, :pair_w] * (1.0 / ol[:, pair_w:])
        o = jnp.where(first, on[:lq], on[lq:])
        o_ref[:, sl] = (_silu(z[:, sl]) * o).astype(o_ref.dtype)


def _attn_prompt_kernel(q_ref, k_ref, v_ref, z_ref, bias_ref, o_ref, kpad_ref, vpad_ref):
    step = pl.program_id(1)
    pad = BAND * CHUNK
    rows = q_ref.shape[0]
    nk = pad + rows

    @pl.when(step == 0)
    def _():
        kpad_ref[0:pad, :] = jnp.zeros((pad, D_GROUP), BF16)
        vpad_ref[0:pad, :] = jnp.zeros((pad, D_GROUP), BF16)
        kpad_ref[pad:, :] = k_ref[...].astype(BF16)
        vpad_ref[pad:, :] = v_ref[...].astype(BF16)

    start = pl.multiple_of(step * rows, rows)
    k_all = kpad_ref[pl.ds(start, nk), :]
    v_all = vpad_ref[pl.ds(start, nk), :]
    _attn_heads(q_ref[...].astype(F32), z_ref[...].astype(F32), k_all, v_all, bias_ref,
                pad - step * rows, o_ref)


def _attn_prompt(u, bias, batch, seq):
    rows, nk = bias.shape[1], bias.shape[2]
    ns = seq // rows
    return pl.pallas_call(
        _attn_prompt_kernel,
        grid=(batch, ns),
        in_specs=[
            pl.BlockSpec((rows, D_GROUP), lambda b, s: (b * ns + s, COL_AQ)),
            pl.BlockSpec((seq, D_GROUP), lambda b, s: (b, COL_AK)),
            pl.BlockSpec((seq, D_GROUP), lambda b, s: (b, COL_AV)),
            pl.BlockSpec((rows, D_GROUP), lambda b, s: (b * ns + s, COL_AZ)),
            pl.BlockSpec((ATT_HEADS, rows, nk), lambda b, s: (0, 0, 0)),
        ],
        out_specs=pl.BlockSpec((rows, D_GROUP), lambda b, s: (b * ns + s, 0)),
        out_shape=jax.ShapeDtypeStruct((batch * seq, D_GROUP), BF16),
        scratch_shapes=[pltpu.VMEM((seq + BAND * CHUNK, D_GROUP), BF16),
                        pltpu.VMEM((seq + BAND * CHUNK, D_GROUP), BF16)],
        compiler_params=_params(2),
        name="attn_prompt",
    )(u, u, u, u, bias)


def _attn_sample_kernel(q_ref, k_ref, v_ref, z_ref, kc_ref, vc_ref, bias_ref,
                        o_ref, kn_ref, vn_ref, kall_ref, vall_ref):
    w = kc_ref.shape[1]
    t = q_ref.shape[0]
    kall_ref[0:w, :] = kc_ref[0].astype(BF16)
    vall_ref[0:w, :] = vc_ref[0].astype(BF16)
    kall_ref[w:, :] = k_ref[...].astype(BF16)
    vall_ref[w:, :] = v_ref[...].astype(BF16)
    kn_ref[0, 0:w - t, :] = kc_ref[0, t:, :]
    vn_ref[0, 0:w - t, :] = vc_ref[0, t:, :]
    kn_ref[0, w - t:, :] = k_ref[...].astype(F32)
    vn_ref[0, w - t:, :] = v_ref[...].astype(F32)
    _attn_heads(q_ref[...].astype(F32), z_ref[...].astype(F32), kall_ref[...], vall_ref[...], bias_ref, None, o_ref)


def _attn_sample(u, k_cache_all, v_cache_all, layer, bias, batch, seq):
    w = k_cache_all.shape[2]
    tok = pl.BlockSpec((seq, D_GROUP), lambda b: (b, 0))
    cache = pl.BlockSpec((1, w, D_GROUP), lambda b: (b, 0, 0))
    cache_in = pl.BlockSpec((None, 1, w, D_GROUP), lambda b: (layer, b, 0, 0))
    return pl.pallas_call(
        _attn_sample_kernel,
        grid=(batch,),
        in_specs=[
            pl.BlockSpec((seq, D_GROUP), lambda b: (b, COL_AQ)),
            pl.BlockSpec((seq, D_GROUP), lambda b: (b, COL_AK)),
            pl.BlockSpec((seq, D_GROUP), lambda b: (b, COL_AV)),
            pl.BlockSpec((seq, D_GROUP), lambda b: (b, COL_AZ)),
            cache_in, cache_in,
            pl.BlockSpec((ATT_HEADS, seq, w + seq), lambda b: (0, 0, 0)),
        ],
        out_specs=[tok, cache, cache],
        out_shape=[jax.ShapeDtypeStruct((batch * seq, D_GROUP), BF16),
                   jax.ShapeDtypeStruct((batch, w, D_GROUP), F32),
                   jax.ShapeDtypeStruct((batch, w, D_GROUP), F32)],
        scratch_shapes=[pltpu.VMEM((w + seq, D_GROUP), BF16),
                        pltpu.VMEM((w + seq, D_GROUP), BF16)],
        compiler_params=_params(1),
        name="attn_sample",
    )(u, u, u, u, k_cache_all, v_cache_all, bias)


def _mlstm_kernel(q_ref, k_ref, v_ref, o_ref, z_ref, gc_ref, gr_ref, bc_ref, br_ref,
                  c0_ref, n0_ref, m0_ref,
                  y_ref, cout_ref, nout_ref, mout_ref, c_s, n_s, m_s):
    c = pl.program_id(1)
    L = q_ref.shape[0]

    @pl.when(c == 0)
    def _():
        c_s[...] = c0_ref[0]
        n_s[...] = n0_ref[0]
        m_s[...] = m0_ref[0]

    row = lax.broadcasted_iota(jnp.int32, (L, L), 0)
    col = lax.broadcasted_iota(jnp.int32, (L, L), 1)
    tril = (row >= col).astype(F32)
    triu = (row <= col).astype(F32)
    causal = row >= col
    ones_lv = jnp.ones((L, ML_DH), BF16)
    eye_bf = (lax.broadcasted_iota(jnp.int32, (ML_DH, ML_DH), 0)
              == lax.broadcasted_iota(jnp.int32, (ML_DH, ML_DH), 1)).astype(BF16)

    g_col = gc_ref[...] + bc_ref[...]
    g_row = gr_ref[0, 0] + br_ref[...]
    b_col = _dot_exact(tril, _log_sigmoid(g_col))
    b_row = _dot_exact(_log_sigmoid(g_row), triu)

    m_all, n_all = m_s[...], n_s[...]
    c_all = [c_s[h] for h in range(ML_HEADS)]
    c_new, n_new, m_new_rows = [], [], []

    def over_keys(x):
        return x[:, :L] if L <= ML_DH else jnp.concatenate([x] * (L // ML_DH), axis=1)

    for h in range(ML_HEADS):
        sl = slice(h * ML_DH, (h + 1) * ML_DH)
        bt = jnp.broadcast_to(b_col[:, ML_HEADS + h:ML_HEADS + h + 1], (L, ML_DH))
        ig_t = jnp.broadcast_to(g_col[:, h:h + 1], (L, ML_DH))
        bs = b_row[ML_HEADS + h:ML_HEADS + h + 1, :]
        ig_s = g_row[h:h + 1, :]
        m_prev = m_all[h:h + 1, :]
        n_prev = n_all[h:h + 1, :]
        c_prev = c_all[h]

        qh = q_ref[:, sl].astype(F32)
        kh = k_ref[:, sl].astype(F32) * (ML_DH ** -0.5)
        vh = v_ref[:, sl].astype(F32)
        qb, kb = qh.astype(BF16), kh.astype(BF16)

        d = jnp.where(causal, over_keys(bt) - bs + ig_s, -jnp.inf)
        m_inter = bt + m_prev
        m_t = jnp.maximum(m_inter, jnp.broadcast_to(jnp.max(d, axis=-1, keepdims=True), (L, ML_DH)))
        w_intra = jnp.exp(d - over_keys(m_t)) * _dot_nt(qb, kb)
        w_inter = jnp.exp(m_inter - m_t)
        pv = _dot(w_intra.astype(BF16), jnp.concatenate([vh.astype(BF16), ones_lv], axis=1))
        c_aug = jnp.concatenate([c_prev, jnp.broadcast_to(n_prev, (ML_DH, ML_DH))], axis=0).astype(BF16)
        cq = _dot_nt(qb, c_aug)
        num = w_inter * cq[:, :ML_DH] + pv[:, :ML_DH]
        den = w_inter * cq[:, ML_DH:] + pv[:, ML_DH:]
        hh = num / jnp.maximum(jnp.abs(den), jnp.exp(-m_t))
        ym = _sigmoid(o_ref[:, sl].astype(F32)) * hh
        y_ref[:, sl] = (_silu(z_ref[:, sl].astype(F32)) * ym).astype(y_ref.dtype)

        b_last = bt[L - 1:L, :]
        m_new = jnp.maximum(b_last + m_prev,
                            jnp.max(b_last[:, 0:1] - bs + ig_s, axis=-1, keepdims=True))
        ws = jnp.exp(b_last - bt + ig_t - m_new)
        decay = jnp.exp(b_last + m_prev - m_new)
        vw_t = _dot_nt(eye_bf, (vh * ws).astype(BF16)).astype(BF16)
        c_new.append(decay * c_prev + _dot(vw_t, kb))
        n_new.append(decay * n_prev + jnp.sum(ws * kh, axis=0, keepdims=True))
        m_new_rows.append(m_new)

    for h in range(ML_HEADS):
        c_s[h] = c_new[h]
        cout_ref[0, h] = c_new[h]
    n_cat = jnp.concatenate(n_new, axis=0)
    n_s[...] = n_cat
    nout_ref[0] = n_cat
    m_cat = jnp.concatenate(m_new_rows + [m_all[ML_HEADS:, :]], axis=0)
    m_s[...] = m_cat
    mout_ref[0] = m_cat


def _mlstm(u, gates, gates_t, b_col, b_row, c0, n0, m0, batch, seq, L):
    nc = seq // L

    def tok(colblk):
        return pl.BlockSpec((L, D_GROUP), lambda b, c: (b * nc + c, colblk))

    st_c = pl.BlockSpec((1, ML_HEADS, ML_DH, ML_DH), lambda b, c: (b, 0, 0, 0))
    st_n = pl.BlockSpec((1, ML_HEADS, ML_DH), lambda b, c: (b, 0, 0))
    st_m = pl.BlockSpec((1, SUBLANES, ML_DH), lambda b, c: (b, 0, 0))
    return pl.pallas_call(
        _mlstm_kernel,
        grid=(batch, nc),
        in_specs=[
            tok(COL_MQ), tok(COL_MK), tok(COL_MV), tok(COL_MO), tok(COL_MZ),
            pl.BlockSpec((L, GATE_PAD), lambda b, c: (b * nc + c, 0)),
            pl.BlockSpec((1, 1, SUBLANES, L), lambda b, c: (b, c, 0, 0)),
            pl.BlockSpec((1, GATE_PAD), lambda b, c: (0, 0)),
            pl.BlockSpec((SUBLANES, 1), lambda b, c: (0, 0)),
            st_c, st_n, st_m,
        ],
        out_specs=[pl.BlockSpec((L, D_GROUP), lambda b, c: (b * nc + c, 0)), st_c, st_n, st_m],
        out_shape=[jax.ShapeDtypeStruct((batch * seq, D_GROUP), BF16),
                   jax.ShapeDtypeStruct((batch, ML_HEADS, ML_DH, ML_DH), F32),
                   jax.ShapeDtypeStruct((batch, ML_HEADS, ML_DH), F32),
                   jax.ShapeDtypeStruct((batch, SUBLANES, ML_DH), F32)],
        scratch_shapes=[pltpu.VMEM((ML_HEADS, ML_DH, ML_DH), F32),
                        pltpu.VMEM((ML_HEADS, ML_DH), F32),
                        pltpu.VMEM((SUBLANES, ML_DH), F32)],
        compiler_params=_params(2),
        name="mlstm",
    )(u, u, u, u, u, gates, gates_t, b_col, b_row, c0, n0, m0)


def _s5_kernel(u_ref, z_ref, bd_ref, cd_ref, are_ref, aim_ref, dskip_ref, wglu_ref, bglu_ref,
               s0re_ref, s0im_ref, y_ref, sre_ref, sim_ref, perm_s, up_s, xr_s, xi_s, car_s, cai_s):
    t = pl.program_id(1)
    nseq, T = u_ref.shape[0], u_ref.shape[1]
    rows = nseq * T
    NS = S5_BLK_STATES
    ntile = D_GROUP // LANES

    @pl.when(t == 0)
    def _():
        car_s[...] = s0re_ref[...]
        cai_s[...] = s0im_ref[...]

    u_seq = u_ref[...].astype(F32).reshape(rows, D_GROUP)
    for j in range(ntile):
        perm_s[j] = u_seq[:, j * LANES:(j + 1) * LANES]

    def to_frame_major(g, carry):
        dst = pl.ds(pl.multiple_of(g * nseq, nseq), nseq)
        for j in range(ntile):
            up_s[dst, j * LANES:(j + 1) * LANES] = perm_s[j, pl.ds(g, nseq, stride=T), :]
        return carry

    lax.fori_loop(0, T, to_frame_major, 0, unroll=8)
    u = up_s[...]
    ub = u.astype(BF16)
    for blk in range(S5_BLOCKS):
        cols = slice(blk * NS, (blk + 1) * NS)
        bu = _dot(ub[:, blk * 128:(blk + 1) * 128], bd_ref[blk])
        xr_s[:, cols] = bu[:, :NS]
        xi_s[:, cols] = bu[:, NS:]

    for blk in range(S5_BLOCKS):
        cols = slice(blk * NS, (blk + 1) * NS)
        ar = are_ref[:, cols]
        ai = aim_ref[:, cols]

        def body(g, carry, cols=cols, ar=ar, ai=ai):
            xr, xi = carry
            frame = pl.ds(pl.multiple_of(g * nseq, nseq), nseq)
            nr = ar * xr - ai * xi + xr_s[frame, cols]
            ni = ar * xi + ai * xr + xi_s[frame, cols]
            xr_s[frame, cols] = nr
            xi_s[frame, cols] = ni
            return nr, ni

        cr, ci = lax.fori_loop(0, T, body, (car_s[:, cols], cai_s[:, cols]), unroll=4)
        car_s[:, cols] = cr
        cai_s[:, cols] = ci

    ys = []
    for blk in range(S5_BLOCKS):
        cols = slice(blk * NS, (blk + 1) * NS)
        ys.append(_dot(xr_s[:, cols].astype(BF16), cd_ref[blk, :NS, :])
                  + _dot(xi_s[:, cols].astype(BF16), cd_ref[blk, NS:, :]))
    y = _gelu_tanh(jnp.concatenate(ys, axis=1) + dskip_ref[...] * u)
    y = y * _sigmoid(_dot(y.astype(BF16), wglu_ref[...]) + bglu_ref[...])
    for j in range(ntile):
        perm_s[j] = y[:, j * LANES:(j + 1) * LANES]
    for i in range(nseq):
        y_i = jnp.concatenate([perm_s[j, pl.ds(i, T, stride=nseq), :] for j in range(ntile)], axis=1)
        y_ref[i] = (_silu(z_ref[i].astype(F32)) * y_i).astype(y_ref.dtype)
    sre_ref[...] = car_s[...]
    sim_ref[...] = cai_s[...]


def _s5(u, bd, cd, a_re, a_im, dskip, wglu, bglu, s0_re, s0_im, batch, seq, T):
    nseq = SUBLANES
    nt = seq // T
    ns_all = S5_GROUPS * S5_STATE
    u3 = u.reshape(batch, seq, N_MAIN)

    def tok(colblk):
        return pl.BlockSpec((nseq, T, D_GROUP), lambda b, t: (b, t, colblk))

    def full(shape):
        return pl.BlockSpec(shape, lambda b, t: (0,) * len(shape))

    st = pl.BlockSpec((nseq, ns_all), lambda b, t: (b, 0))
    y, s_re, s_im = pl.pallas_call(
        _s5_kernel,
        grid=(batch // nseq, nt),
        in_specs=[tok(COL_SU), tok(COL_SZ),
                  full(bd.shape), full(cd.shape), full(a_re.shape), full(a_im.shape),
                  full(dskip.shape), full(wglu.shape), full(bglu.shape), st, st],
        out_specs=[pl.BlockSpec((nseq, T, D_GROUP), lambda b, t: (b, t, 0)), st, st],
        out_shape=[jax.ShapeDtypeStruct((batch, seq, D_GROUP), BF16),
                   jax.ShapeDtypeStruct((batch, ns_all), F32),
                   jax.ShapeDtypeStruct((batch, ns_all), F32)],
        scratch_shapes=[pltpu.VMEM((D_GROUP // LANES, nseq * T, LANES), F32),
                        pltpu.VMEM((nseq * T, D_GROUP), F32),
                        pltpu.VMEM((nseq * T, ns_all), F32), pltpu.VMEM((nseq * T, ns_all), F32),
                        pltpu.VMEM((nseq, ns_all), F32), pltpu.VMEM((nseq, ns_all), F32)],
        compiler_params=_params(2),
        name="s5",
    )(u3, u3, bd, cd, a_re, a_im, dskip, wglu, bglu, s0_re, s0_im)
    return y.reshape(batch * seq, D_GROUP), s_re, s_im


def _rel_bias_table(rel_bias, n_q, n_k, offset, band):
    length = n_k + n_q - 1
    j = jnp.arange(length)
    idx = jnp.clip(offset + n_q - 1 - j, -REL_CLIP, REL_CLIP) + REL_CLIP
    ext = jnp.pad(rel_bias[..., idx].astype(F32), [(0, 0)] * (rel_bias.ndim - 1) + [(0, 1)])
    lead = ext.shape[:-1]
    flat = jnp.tile(ext, n_q)[..., :n_q * length]
    tbl = flat.reshape(lead + (n_q, length))[..., n_q - 1:n_q - 1 + n_k]
    if band:
        qc = jnp.arange(n_q)[:, None] // CHUNK
        kc = jnp.arange(n_k)[None, :] // CHUNK
        tbl = jnp.where((kc >= qc) & (kc <= qc + BAND), tbl, -jnp.inf)
    return tbl


def _s5_params(a_re, a_im, log_dt, b_re, b_im, c_re, c_im):
    dt = jnp.exp(log_dt)[:, None]
    mag = jnp.exp(a_re * dt)
    ab_re, ab_im = mag * jnp.cos(a_im * dt), mag * jnp.sin(a_im * dt)
    den = a_re * a_re + a_im * a_im
    co_re = ((ab_re - 1.0) * a_re + ab_im * a_im) / den
    co_im = (ab_im * a_re - (ab_re - 1.0) * a_im) / den
    bb_re = co_re[..., None] * b_re - co_im[..., None] * b_im
    bb_im = co_re[..., None] * b_im + co_im[..., None] * b_re
    a8_re = jnp.broadcast_to(ab_re.reshape(1, -1), (SUBLANES, S5_GROUPS * S5_STATE))
    a8_im = jnp.broadcast_to(ab_im.reshape(1, -1), (SUBLANES, S5_GROUPS * S5_STATE))
    gpb = S5_GROUPS // S5_BLOCKS
    eye = jnp.eye(gpb, dtype=F32)

    def blockdiag_in(bb):
        bbk = bb.reshape(S5_BLOCKS, gpb, S5_STATE, S5_CH)
        return jnp.einsum('kgpc,gh->kgchp', bbk, eye).reshape(S5_BLOCKS, gpb * S5_CH, gpb * S5_STATE)

    def blockdiag_out(cc):
        cck = cc.reshape(S5_BLOCKS, gpb, S5_CH, S5_STATE)
        return jnp.einsum('kgcp,gh->kgphc', cck, eye).reshape(S5_BLOCKS, gpb * S5_STATE, gpb * S5_CH)

    bd = jnp.concatenate([blockdiag_in(bb_re), blockdiag_in(bb_im)], axis=-1).astype(BF16)
    cd = jnp.concatenate([blockdiag_out(c_re), blockdiag_out(-c_im)], axis=1).astype(BF16)
    return bd, cd, a8_re, a8_im


def _tiles(batch, seq):
    return (min(batch * seq, IN_PROJ_ROWS), IN_PROJ_COLS, min(seq, OUT_PROJ_ROWS),
            min(seq, MLSTM_CHUNK), min(seq, S5_FRAMES))


def _layer(x, xn, batch, seq, att_cache, ml_state, conv_state, s5_state, p, layer, last, tiles):
    tm_in, tn_in, tm_out, L, t_s5 = tiles
    u, gates = _inproj(xn, p["w_head"], p["w_tail"], p["w_gate"], layer, tm_in, tn_in)
    nc = seq // L
    gates_t = gates[:, :SUBLANES].reshape(batch, nc, L, SUBLANES).transpose(0, 1, 3, 2)

    if att_cache is None:
        ya = _attn_prompt(u, p["bias"], batch, seq)
        rows = min(BAND * CHUNK, seq)
        u3 = u.reshape(batch, seq, N_MAIN)
        new_k = u3[:, seq - rows:, COL_AK * D_GROUP:(COL_AK + 1) * D_GROUP].astype(F32)
        new_v = u3[:, seq - rows:, COL_AV * D_GROUP:(COL_AV + 1) * D_GROUP].astype(F32)
    else:
        ya, new_k, new_v = _attn_sample(u, att_cache[0], att_cache[1], layer, p["bias"], batch, seq)
    new_k = new_k.reshape(batch, -1, ATT_HEADS, ATT_DH)
    new_v = new_v.reshape(batch, -1, ATT_HEADS, ATT_DH)

    c0, n0, m0 = ml_state
    m0 = jnp.broadcast_to(jnp.pad(m0, ((0, 0), (0, SUBLANES - ML_HEADS)))[:, :, None],
                          (batch, SUBLANES, ML_DH))
    ym, c_new, n_new, m_new = _mlstm(u, gates, gates_t, p["b_col"], p["b_row"], c0, n0, m0, batch, seq, L)
    m_new = m_new[:, :ML_HEADS, 0]

    ys, s5_re, s5_im = _s5(u, p["bd"], p["cd"], p["a_re"], p["a_im"], p["dskip"], p["wglu"], p["bglu"],
                           s5_state[0].reshape(batch, -1), s5_state[1].reshape(batch, -1),
                           batch, seq, t_s5)
    s5_re = s5_re.reshape(batch, S5_GROUPS, S5_STATE)
    s5_im = s5_im.reshape(batch, S5_GROUPS, S5_STATE)

    *outs, conv_new = _outproj(ya, ym, ys, u, p["conv_w"], conv_state, x, p["w_out_all"], layer, p["g_next"],
                               last, batch, seq, tm_out)
    return outs, (new_k, new_v, c_new, n_new, m_new, conv_new, s5_re, s5_im)


def kernel(x_prompt, x_sample, cache_attn_k, cache_attn_v, state_mlstm_C, state_mlstm_n, state_mlstm_m, state_conv, state_s5_re, state_s5_im, norm_g, w_in, w_out, attn_rel_bias, mlstm_b_if, conv_w, s5_A_re, s5_A_im, s5_log_dt, s5_B_re, s5_B_im, s5_C_re, s5_C_im, s5_D, s5_w_glu, s5_b_glu, final_norm_g):
    depth = w_in.shape[0]
    bp, sp, _ = x_prompt.shape
    bs, ss, _ = x_sample.shape
    w_rows = cache_attn_k.shape[2]
    gate_lo = 9 * D_GROUP
    gate_hi = gate_lo + 2 * ML_HEADS

    xp = x_prompt.reshape(bp * sp, D_MODEL)
    xs = x_sample.reshape(bs * ss, D_MODEL)
    ml0 = (jnp.zeros((bp, ML_HEADS, ML_DH, ML_DH), F32), jnp.zeros((bp, ML_HEADS, ML_DH), F32),
           jnp.zeros((bp, ML_HEADS), F32))
    conv0 = jnp.zeros((bp, CONV_W - 1, D_GROUP), F32)
    s50 = (jnp.zeros((bp, S5_GROUPS, S5_STATE), F32), jnp.zeros((bp, S5_GROUPS, S5_STATE), F32))

    tiles_p = _tiles(bp, sp)
    tiles_s = _tiles(bs, ss)
    att_rows = ATT_CHUNKS_PER_STEP * CHUNK
    bias_p = _rel_bias_table(attn_rel_bias, att_rows, BAND * CHUNK + att_rows, BAND * CHUNK, True)
    bias_s = _rel_bias_table(attn_rel_bias, ss, w_rows + ss, w_rows, False)
    w_head = jnp.swapaxes(w_in, 1, 2).astype(BF16)
    w_tail = w_head[:, gate_hi:]
    w_gate = jnp.pad(w_head[:, gate_lo:gate_hi], ((0, 0), (0, GATE_PAD - 2 * ML_HEADS), (0, 0)))
    w_out_all = w_out.astype(BF16)
    g_all = jnp.concatenate([norm_g, final_norm_g[None]], axis=0).reshape(depth + 1, 1, D_MODEL)
    xnp = _prenorm(xp, g_all[0], tiles_p[0])
    xns = _prenorm(xs, g_all[0], bs * ss)

    st_p, st_s = [], []
    for l in range(depth):
        bd, cd, a8_re, a8_im = _s5_params(s5_A_re[l], s5_A_im[l], s5_log_dt[l], s5_B_re[l], s5_B_im[l],
                                          s5_C_re[l], s5_C_im[l])
        b_if = mlstm_b_if[l]
        last = l == depth - 1
        p = {
            "w_head": w_head, "w_tail": w_tail, "w_gate": w_gate, "w_out_all": w_out_all, "g_next": g_all[l + 1],
            "b_col": jnp.pad(b_if, (0, GATE_PAD - 2 * ML_HEADS)).reshape(1, GATE_PAD),
            "b_row": b_if.reshape(SUBLANES, 1),
            "conv_w": conv_w[l],
            "bd": bd, "cd": cd, "a_re": a8_re, "a_im": a8_im,
            "dskip": s5_D[l].reshape(1, D_GROUP),
            "wglu": s5_w_glu[l].astype(BF16),
            "bglu": s5_b_glu[l].reshape(1, D_GROUP),
        }
        p_prompt = dict(p, bias=bias_p[l])
        p_sample = dict(p, bias=bias_s[l])

        out_p, sp_l = _layer(xp, xnp, bp, sp, None, ml0, conv0, s50, p_prompt, l, last, tiles_p)
        out_s, ss_l = _layer(xs, xns, bs, ss,
                             (cache_attn_k.reshape(depth, bs, w_rows, D_GROUP),
                              cache_attn_v.reshape(depth, bs, w_rows, D_GROUP)),
                             (state_mlstm_C[l], state_mlstm_n[l], state_mlstm_m[l]),
                             state_conv[l], (state_s5_re[l], state_s5_im[l]), p_sample, l, last, tiles_s)
        st_p.append(sp_l)
        st_s.append(ss_l)
        if not last:
            (xp, xnp), (xs, xns) = out_p, out_s

    y_prompt = out_p[0].reshape(bp, sp, D_MODEL)
    y_sample = out_s[0].reshape(bs, ss, D_MODEL)
    outs_p = [jnp.stack(t) for t in zip(*st_p)]
    outs_s = [jnp.stack(t) for t in zip(*st_s)]
    return (y_prompt, y_sample, *outs_p, *outs_s)
```

```python
import functools
import math

import jax
import jax.numpy as jnp
from jax import lax
from jax.experimental import pallas as pl
from jax.experimental.pallas import tpu as pltpu

F32 = jnp.float32
BF16 = jnp.bfloat16

D_MODEL = 2048
D_GROUP = 512
CHUNK = 64
BAND = 8
REL_CLIP = 128
ATT_HEADS = 8
ATT_DH = 64
ML_HEADS = 4
ML_DH = 128
CONV_W = 3
S5_GROUPS = 32
S5_CH = 16
S5_STATE = 64
S5_BLOCKS = 4
S5_BLK_STATES = S5_GROUPS * S5_STATE // S5_BLOCKS
NORM_EPS = 1e-6
N_MAIN = 15 * D_GROUP
N_HEAD = 9 * D_GROUP
GATE_PAD = 128
SUBLANES = 8
LANES = 128
IN_PROJ_ROWS = 1024
IN_PROJ_COLS = 3 * D_GROUP
OUT_PROJ_ROWS = 512
MLSTM_CHUNK = 4 * CHUNK
S5_FRAMES = 128
ATT_CHUNKS_PER_STEP = 4
VMEM_LIMIT = 56 * 1024 * 1024

COL_AQ, COL_AK, COL_AV, COL_AZ = 0, 1, 2, 3
COL_MQ, COL_MK, COL_MV, COL_MO, COL_MZ = 4, 5, 6, 7, 8
COL_CB, COL_CC, COL_CX, COL_CZ = 9, 10, 11, 12
COL_SU, COL_SZ = 13, 14


def _params(n_axes):
    return pltpu.CompilerParams(dimension_semantics=("arbitrary",) * n_axes,
                                vmem_limit_bytes=VMEM_LIMIT)


def _sigmoid(x):
    return 1.0 / (1.0 + jnp.exp(-x))


def _silu(x):
    return x * _sigmoid(x)


def _log_sigmoid(x):
    return jnp.minimum(x, 0.0) - jnp.log(1.0 + jnp.exp(-jnp.abs(x)))


def _gelu_tanh(x):
    return 0.5 * x * (1.0 + jnp.tanh(math.sqrt(2.0 / math.pi) * (x + 0.044715 * (x * x * x))))


def _rms(x, g):
    return x * lax.rsqrt(jnp.mean(x * x, axis=-1, keepdims=True) + NORM_EPS) * g


def _dot(a, b):
    return jnp.dot(a, b, preferred_element_type=F32)


def _dot_nt(a, b):
    return lax.dot_general(a, b, (((1,), (1,)), ((), ())), preferred_element_type=F32)


def _dot_exact(a, b):
    return jnp.dot(a, b, preferred_element_type=F32, precision=lax.Precision.HIGHEST)


def _prenorm_kernel(x_ref, g_ref, o_ref):
    o_ref[...] = _rms(x_ref[...], g_ref[...]).astype(o_ref.dtype)


def _prenorm(x, g, tm):
    n = x.shape[0]
    return pl.pallas_call(
        _prenorm_kernel,
        grid=(n // tm,),
        in_specs=[pl.BlockSpec((tm, D_MODEL), lambda i: (i, 0)),
                  pl.BlockSpec((1, D_MODEL), lambda i: (0, 0))],
        out_specs=pl.BlockSpec((tm, D_MODEL), lambda i: (i, 0)),
        out_shape=jax.ShapeDtypeStruct((n, D_MODEL), BF16),
        compiler_params=_params(1),
        name="prenorm",
    )(x, g)


def _inproj_kernel(xn_ref, wh_ref, wt_ref, wg_ref, u_ref, gate_ref, *, head_tiles):
    j = pl.program_id(1)

    @pl.when(j == 0)
    def _():
        gate_ref[...] = _dot_nt(xn_ref[...], wg_ref[...])

    @pl.when(j < head_tiles)
    def _():
        u_ref[...] = _dot_nt(xn_ref[...], wh_ref[...]).astype(u_ref.dtype)

    @pl.when(j >= head_tiles)
    def _():
        u_ref[...] = _dot_nt(xn_ref[...], wt_ref[...]).astype(u_ref.dtype)


def _inproj(xn, w_head, w_tail, w_gate, layer, tm, tn):
    n = xn.shape[0]
    head_tiles = N_HEAD // tn
    return pl.pallas_call(
        functools.partial(_inproj_kernel, head_tiles=head_tiles),
        grid=(n // tm, N_MAIN // tn),
        in_specs=[
            pl.BlockSpec((tm, D_MODEL), lambda i, j: (i, 0)),
            pl.BlockSpec((None, tn, D_MODEL), lambda i, j: (layer, jnp.minimum(j, head_tiles - 1), 0)),
            pl.BlockSpec((None, tn, D_MODEL), lambda i, j: (layer, jnp.maximum(j - head_tiles, 0), 0)),
            pl.BlockSpec((None, GATE_PAD, D_MODEL), lambda i, j: (layer, 0, 0)),
        ],
        out_specs=[
            pl.BlockSpec((tm, tn), lambda i, j: (i, j)),
            pl.BlockSpec((tm, GATE_PAD), lambda i, j: (i, 0)),
        ],
        out_shape=[
            jax.ShapeDtypeStruct((n, N_MAIN), BF16),
            jax.ShapeDtypeStruct((n, GATE_PAD), F32),
        ],
        compiler_params=_params(2),
        name="inproj",
    )(xn, w_head, w_tail, w_gate)


def _outproj_kernel(ya_ref, ym_ref, ys_ref, cb_ref, cc_ref, cx_ref, cz_ref, cw_ref, cs0_ref,
                    x_ref, w_ref, g_ref, *refs, tiles_per_seq):
    *out_refs, cs_ref, u_s = refs
    tm = x_ref.shape[0]
    head = SUBLANES
    taps = CONV_W - 1

    @pl.when(pl.program_id(0) % tiles_per_seq == 0)
    def _():
        u_s[head - taps:head, :] = cs0_ref[0]

    u_s[head:, :] = cc_ref[...].astype(F32) * cx_ref[...].astype(F32)
    y = cw_ref[0:1, :] * u_s[head - 2:head - 2 + tm, :]
    y = y + cw_ref[1:2, :] * u_s[head - 1:head - 1 + tm, :]
    y = y + cw_ref[2:3, :] * u_s[head:, :]
    yc = (_silu(cz_ref[...].astype(F32)) * (cb_ref[...].astype(F32) * y)).astype(BF16)
    last_rows = u_s[head + tm - taps:, :]
    cs_ref[0] = last_rows
    u_s[head - taps:head, :] = last_rows

    acc = x_ref[...] + _dot(jnp.concatenate([ya_ref[...], ym_ref[...], yc, ys_ref[...]], axis=1), w_ref[...])
    normed = _rms(acc, g_ref[...])
    if len(out_refs) == 2:
        out_refs[0][...] = acc
    out_refs[-1][...] = normed.astype(out_refs[-1].dtype)


def _outproj(ya, ym, ys, u, conv_w, conv_s0, x, w_out_all, layer, g_next, last, batch, seq, tm):
    n = x.shape[0]
    tps = seq // tm
    yspec = pl.BlockSpec((tm, D_GROUP), lambda i: (i, 0))
    row = pl.BlockSpec((tm, D_MODEL), lambda i: (i, 0))
    st = pl.BlockSpec((1, CONV_W - 1, D_GROUP), lambda i: (i // tps, 0, 0))

    def tok(colblk):
        return pl.BlockSpec((tm, D_GROUP), lambda i: (i, colblk))

    out_specs, out_shape = [row], [jax.ShapeDtypeStruct((n, D_MODEL), F32)]
    if not last:
        out_specs.append(row)
        out_shape.append(jax.ShapeDtypeStruct((n, D_MODEL), BF16))
    out_specs.append(st)
    out_shape.append(jax.ShapeDtypeStruct((batch, CONV_W - 1, D_GROUP), F32))
    return pl.pallas_call(
        functools.partial(_outproj_kernel, tiles_per_seq=tps),
        grid=(n // tm,),
        in_specs=[yspec, yspec, yspec, tok(COL_CB), tok(COL_CC), tok(COL_CX), tok(COL_CZ),
                  pl.BlockSpec((CONV_W, D_GROUP), lambda i: (0, 0)), st, row,
                  pl.BlockSpec((None, D_MODEL, D_MODEL), lambda i: (layer, 0, 0)),
                  pl.BlockSpec((1, D_MODEL), lambda i: (0, 0))],
        out_specs=out_specs,
        out_shape=out_shape,
        scratch_shapes=[pltpu.VMEM((SUBLANES + tm, D_GROUP), F32)],
        compiler_params=_params(1),
        name="outproj",
    )(ya, ym, ys, u, u, u, u, conv_w, conv_s0, x, w_out_all, g_next)


def _attn_heads(q, z, k_all, v_all, bias_ref, first_key, o_ref):
    lq, nk = q.shape[0], k_all.shape[0]
    pair_w = 2 * ATT_DH
    first = lax.broadcasted_iota(jnp.int32, (lq, pair_w), 1) < ATT_DH
    kcol = lax.broadcasted_iota(jnp.int32, (2 * lq, nk), 1)
    ones = jnp.ones((nk, pair_w), BF16)
    for pair in range(ATT_HEADS // 2):
        sl = slice(pair * pair_w, (pair + 1) * pair_w)
        qp = q[:, sl] * (ATT_DH ** -0.5)
        kp, vp = k_all[:, sl], v_all[:, sl]
        q2 = jnp.concatenate([jnp.where(first, qp, 0.0), jnp.where(first, 0.0, qp)], axis=0).astype(BF16)
        s = _dot_nt(q2, kp) + jnp.concatenate([bias_ref[2 * pair], bias_ref[2 * pair + 1]], axis=0)
        if first_key is not None:
            s = jnp.where(kcol >= first_key, s, -jnp.inf)
        p = jnp.exp(s - jnp.max(s, axis=-1, keepdims=True)).astype(BF16)
        ol = _dot(p, jnp.concatenate([vp, ones], axis=1))
        on = ol[:, :pair_w] * (1.0 / ol[:, pair_w:])
        o = jnp.where(first, on[:lq], on[lq:])
        o_ref[:, sl] = (_silu(z[:, sl]) * o).astype(o_ref.dtype)


def _attn_prompt_kernel(q_ref, k_ref, v_ref, z_ref, bias_ref, o_ref, kpad_ref, vpad_ref):
    step = pl.program_id(1)
    pad = BAND * CHUNK
    rows = q_ref.shape[0]
    nk = pad + rows

    @pl.when(step == 0)
    def _():
        kpad_ref[0:pad, :] = jnp.zeros((pad, D_GROUP), BF16)
        vpad_ref[0:pad, :] = jnp.zeros((pad, D_GROUP), BF16)
        kpad_ref[pad:, :] = k_ref[...].astype(BF16)
        vpad_ref[pad:, :] = v_ref[...].astype(BF16)

    start = pl.multiple_of(step * rows, rows)
    k_all = kpad_ref[pl.ds(start, nk), :]
    v_all = vpad_ref[pl.ds(start, nk), :]
    _attn_heads(q_ref[...].astype(F32), z_ref[...].astype(F32), k_all, v_all, bias_ref,
                pad - step * rows, o_ref)


def _attn_prompt(u, bias, batch, seq):
    rows, nk = bias.shape[1], bias.shape[2]
    ns = seq // rows
    return pl.pallas_call(
        _attn_prompt_kernel,
        grid=(batch, ns),
        in_specs=[
            pl.BlockSpec((rows, D_GROUP), lambda b, s: (b * ns + s, COL_AQ)),
            pl.BlockSpec((seq, D_GROUP), lambda b, s: (b, COL_AK)),
            pl.BlockSpec((seq, D_GROUP), lambda b, s: (b, COL_AV)),
            pl.BlockSpec((rows, D_GROUP), lambda b, s: (b * ns + s, COL_AZ)),
            pl.BlockSpec((ATT_HEADS, rows, nk), lambda b, s: (0, 0, 0)),
        ],
        out_specs=pl.BlockSpec((rows, D_GROUP), lambda b, s: (b * ns + s, 0)),
        out_shape=jax.ShapeDtypeStruct((batch * seq, D_GROUP), BF16),
        scratch_shapes=[pltpu.VMEM((seq + BAND * CHUNK, D_GROUP), BF16),
                        pltpu.VMEM((seq + BAND * CHUNK, D_GROUP), BF16)],
        compiler_params=_params(2),
        name="attn_prompt",
    )(u, u, u, u, bias)


def _attn_sample_kernel(q_ref, k_ref, v_ref, z_ref, kc_ref, vc_ref, bias_ref,
                        o_ref, kn_ref, vn_ref, kall_ref, vall_ref):
    w = kc_ref.shape[1]
    t = q_ref.shape[0]
    kall_ref[0:w, :] = kc_ref[0].astype(BF16)
    vall_ref[0:w, :] = vc_ref[0].astype(BF16)
    kall_ref[w:, :] = k_ref[...].astype(BF16)
    vall_ref[w:, :] = v_ref[...].astype(BF16)
    kn_ref[0, 0:w - t, :] = kc_ref[0, t:, :]
    vn_ref[0, 0:w - t, :] = vc_ref[0, t:, :]
    kn_ref[0, w - t:, :] = k_ref[...].astype(F32)
    vn_ref[0, w - t:, :] = v_ref[...].astype(F32)
    _attn_heads(q_ref[...].astype(F32), z_ref[...].astype(F32), kall_ref[...], vall_ref[...], bias_ref, None, o_ref)


def _attn_sample(u, k_cache_all, v_cache_all, layer, bias, batch, seq):
    w = k_cache_all.shape[2]
    tok = pl.BlockSpec((seq, D_GROUP), lambda b: (b, 0))
    cache = pl.BlockSpec((1, w, D_GROUP), lambda b: (b, 0, 0))
    cache_in = pl.BlockSpec((None, 1, w, D_GROUP), lambda b: (layer, b, 0, 0))
    return pl.pallas_call(
        _attn_sample_kernel,
        grid=(batch,),
        in_specs=[
            pl.BlockSpec((seq, D_GROUP), lambda b: (b, COL_AQ)),
            pl.BlockSpec((seq, D_GROUP), lambda b: (b, COL_AK)),
            pl.BlockSpec((seq, D_GROUP), lambda b: (b, COL_AV)),
            pl.BlockSpec((seq, D_GROUP), lambda b: (b, COL_AZ)),
            cache_in, cache_in,
            pl.BlockSpec((ATT_HEADS, seq, w + seq), lambda b: (0, 0, 0)),
        ],
        out_specs=[tok, cache, cache],
        out_shape=[jax.ShapeDtypeStruct((batch * seq, D_GROUP), BF16),
                   jax.ShapeDtypeStruct((batch, w, D_GROUP), F32),
                   jax.ShapeDtypeStruct((batch, w, D_GROUP), F32)],
        scratch_shapes=[pltpu.VMEM((w + seq, D_GROUP), BF16),
                        pltpu.VMEM((w + seq, D_GROUP), BF16)],
        compiler_params=_params(1),
        name="attn_sample",
    )(u, u, u, u, k_cache_all, v_cache_all, bias)


def _mlstm_kernel(q_ref, k_ref, v_ref, o_ref, z_ref, gc_ref, gr_ref, bc_ref, br_ref,
                  c0_ref, n0_ref, m0_ref,
                  y_ref, cout_ref, nout_ref, mout_ref, c_s, n_s, m_s):
    c = pl.program_id(1)
    L = q_ref.shape[0]

    @pl.when(c == 0)
    def _():
        c_s[...] = c0_ref[0]
        n_s[...] = n0_ref[0]
        m_s[...] = m0_ref[0]

    row = lax.broadcasted_iota(jnp.int32, (L, L), 0)
    col = lax.broadcasted_iota(jnp.int32, (L, L), 1)
    tril = (row >= col).astype(F32)
    triu = (row <= col).astype(F32)
    causal = row >= col
    ones_lv = jnp.ones((L, ML_DH), BF16)
    eye_bf = (lax.broadcasted_iota(jnp.int32, (ML_DH, ML_DH), 0)
              == lax.broadcasted_iota(jnp.int32, (ML_DH, ML_DH), 1)).astype(BF16)

    g_col = gc_ref[...] + bc_ref[...]
    g_row = gr_ref[0, 0] + br_ref[...]
    b_col = _dot_exact(tril, _log_sigmoid(g_col))
    b_row = _dot_exact(_log_sigmoid(g_row), triu)

    m_all, n_all = m_s[...], n_s[...]
    c_all = [c_s[h] for h in range(ML_HEADS)]
    c_new, n_new, m_new_rows = [], [], []

    def over_keys(x):
        return x[:, :L] if L <= ML_DH else jnp.concatenate([x] * (L // ML_DH), axis=1)

    for h in range(ML_HEADS):
        sl = slice(h * ML_DH, (h + 1) * ML_DH)
        bt = jnp.broadcast_to(b_col[:, ML_HEADS + h:ML_HEADS + h + 1], (L, ML_DH))
        ig_t = jnp.broadcast_to(g_col[:, h:h + 1], (L, ML_DH))
        bs = b_row[ML_HEADS + h:ML_HEADS + h + 1, :]
        ig_s = g_row[h:h + 1, :]
        m_prev = m_all[h:h + 1, :]
        n_prev = n_all[h:h + 1, :]
        c_prev = c_all[h]

        qh = q_ref[:, sl].astype(F32)
        kh = k_ref[:, sl].astype(F32) * (ML_DH ** -0.5)
        vh = v_ref[:, sl].astype(F32)
        qb, kb = qh.astype(BF16), kh.astype(BF16)

        d = jnp.where(causal, over_keys(bt) - bs + ig_s, -jnp.inf)
        m_inter = bt + m_prev
        m_t = jnp.maximum(m_inter, jnp.broadcast_to(jnp.max(d, axis=-1, keepdims=True), (L, ML_DH)))
        w_intra = jnp.exp(d - over_keys(m_t)) * _dot_nt(qb, kb)
        w_inter = jnp.exp(m_inter - m_t)
        pv = _dot(w_intra.astype(BF16), jnp.concatenate([vh.astype(BF16), ones_lv], axis=1))
        c_aug = jnp.concatenate([c_prev, jnp.broadcast_to(n_prev, (ML_DH, ML_DH))], axis=0).astype(BF16)
        cq = _dot_nt(qb, c_aug)
        num = w_inter * cq[:, :ML_DH] + pv[:, :ML_DH]
        den = w_inter * cq[:, ML_DH:] + pv[:, ML_DH:]
        hh = num / jnp.maximum(jnp.abs(den), jnp.exp(-m_t))
        ym = _sigmoid(o_ref[:, sl].astype(F32)) * hh
        y_ref[:, sl] = (_silu(z_ref[:, sl].astype(F32)) * ym).astype(y_ref.dtype)

        b_last = bt[L - 1:L, :]
        m_new = jnp.maximum(b_last + m_prev,
                            jnp.max(b_last[:, 0:1] - bs + ig_s, axis=-1, keepdims=True))
        ws = jnp.exp(b_last - bt + ig_t - m_new)
        decay = jnp.exp(b_last + m_prev - m_new)
        vw_t = _dot_nt(eye_bf, (vh * ws).astype(BF16)).astype(BF16)
        c_new.append(decay * c_prev + _dot(vw_t, kb))
        n_new.append(decay * n_prev + jnp.sum(ws * kh, axis=0, keepdims=True))
        m_new_rows.append(m_new)

    for h in range(ML_HEADS):
        c_s[h] = c_new[h]
        cout_ref[0, h] = c_new[h]
    n_cat = jnp.concatenate(n_new, axis=0)
    n_s[...] = n_cat
    nout_ref[0] = n_cat
    m_cat = jnp.concatenate(m_new_rows + [m_all[ML_HEADS:, :]], axis=0)
    m_s[...] = m_cat
    mout_ref[0] = m_cat


def _mlstm(u, gates, gates_t, b_col, b_row, c0, n0, m0, batch, seq, L):
    nc = seq // L

    def tok(colblk):
        return pl.BlockSpec((L, D_GROUP), lambda b, c: (b * nc + c, colblk))

    st_c = pl.BlockSpec((1, ML_HEADS, ML_DH, ML_DH), lambda b, c: (b, 0, 0, 0))
    st_n = pl.BlockSpec((1, ML_HEADS, ML_DH), lambda b, c: (b, 0, 0))
    st_m = pl.BlockSpec((1, SUBLANES, ML_DH), lambda b, c: (b, 0, 0))
    return pl.pallas_call(
        _mlstm_kernel,
        grid=(batch, nc),
        in_specs=[
            tok(COL_MQ), tok(COL_MK), tok(COL_MV), tok(COL_MO), tok(COL_MZ),
            pl.BlockSpec((L, GATE_PAD), lambda b, c: (b * nc + c, 0)),
            pl.BlockSpec((1, 1, SUBLANES, L), lambda b, c: (b, c, 0, 0)),
            pl.BlockSpec((1, GATE_PAD), lambda b, c: (0, 0)),
            pl.BlockSpec((SUBLANES, 1), lambda b, c: (0, 0)),
            st_c, st_n, st_m,
        ],
        out_specs=[pl.BlockSpec((L, D_GROUP), lambda b, c: (b * nc + c, 0)), st_c, st_n, st_m],
        out_shape=[jax.ShapeDtypeStruct((batch * seq, D_GROUP), BF16),
                   jax.ShapeDtypeStruct((batch, ML_HEADS, ML_DH, ML_DH), F32),
                   jax.ShapeDtypeStruct((batch, ML_HEADS, ML_DH), F32),
                   jax.ShapeDtypeStruct((batch, SUBLANES, ML_DH), F32)],
        scratch_shapes=[pltpu.VMEM((ML_HEADS, ML_DH, ML_DH), F32),
                        pltpu.VMEM((ML_HEADS, ML_DH), F32),
                        pltpu.VMEM((SUBLANES, ML_DH), F32)],
        compiler_params=_params(2),
        name="mlstm",
    )(u, u, u, u, u, gates, gates_t, b_col, b_row, c0, n0, m0)


def _s5_kernel(u_ref, z_ref, bd_ref, cd_ref, are_ref, aim_ref, dskip_ref, wglu_ref, bglu_ref,
               s0re_ref, s0im_ref, y_ref, sre_ref, sim_ref, perm_s, up_s, xr_s, xi_s, car_s, cai_s):
    t = pl.program_id(1)
    nseq, T = u_ref.shape[0], u_ref.shape[1]
    rows = nseq * T
    NS = S5_BLK_STATES
    ntile = D_GROUP // LANES

    @pl.when(t == 0)
    def _():
        car_s[...] = s0re_ref[...]
        cai_s[...] = s0im_ref[...]

    u_seq = u_ref[...].astype(F32).reshape(rows, D_GROUP)
    for j in range(ntile):
        perm_s[j] = u_seq[:, j * LANES:(j + 1) * LANES]

    def to_frame_major(g, carry):
        dst = pl.ds(pl.multiple_of(g * nseq, nseq), nseq)
        for j in range(ntile):
            up_s[dst, j * LANES:(j + 1) * LANES] = perm_s[j, pl.ds(g, nseq, stride=T), :]
        return carry

    lax.fori_loop(0, T, to_frame_major, 0, unroll=8)
    u = up_s[...]
    ub = u.astype(BF16)
    for blk in range(S5_BLOCKS):
        cols = slice(blk * NS, (blk + 1) * NS)
        bu = _dot(ub[:, blk * 128:(blk + 1) * 128], bd_ref[blk])
        xr_s[:, cols] = bu[:, :NS]
        xi_s[:, cols] = bu[:, NS:]

    span = 2 * NS
    for c0 in range(0, S5_BLOCKS * NS, span):
        cols = slice(c0, c0 + span)
        ar = are_ref[:, cols]
        ai = aim_ref[:, cols]

        def body(g, carry, cols=cols, ar=ar, ai=ai):
            xr, xi = carry
            frame = pl.ds(pl.multiple_of(g * nseq, nseq), nseq)
            nr = ar * xr - ai * xi + xr_s[frame, cols]
            ni = ar * xi + ai * xr + xi_s[frame, cols]
            xr_s[frame, cols] = nr
            xi_s[frame, cols] = ni
            return nr, ni

        cr, ci = lax.fori_loop(0, T, body, (car_s[:, cols], cai_s[:, cols]), unroll=4)
        car_s[:, cols] = cr
        cai_s[:, cols] = ci

    ys = []
    for blk in range(S5_BLOCKS):
        cols = slice(blk * NS, (blk + 1) * NS)
        ys.append(_dot(xr_s[:, cols].astype(BF16), cd_ref[blk, :NS, :])
                  + _dot(xi_s[:, cols].astype(BF16), cd_ref[blk, NS:, :]))
    y = _gelu_tanh(jnp.concatenate(ys, axis=1) + dskip_ref[...] * u)
    y = y * _sigmoid(_dot(y.astype(BF16), wglu_ref[...]) + bglu_ref[...])
    for j in range(ntile):
        perm_s[j] = y[:, j * LANES:(j + 1) * LANES]
    for i in range(nseq):
        y_i = jnp.concatenate([perm_s[j, pl.ds(i, T, stride=nseq), :] for j in range(ntile)], axis=1)
        y_ref[i] = (_silu(z_ref[i].astype(F32)) * y_i).astype(y_ref.dtype)
    sre_ref[...] = car_s[...]
    sim_ref[...] = cai_s[...]


def _s5(u, bd, cd, a_re, a_im, dskip, wglu, bglu, s0_re, s0_im, batch, seq, T):
    nseq = SUBLANES
    nt = seq // T
    ns_all = S5_GROUPS * S5_STATE
    u3 = u.reshape(batch, seq, N_MAIN)

    def tok(colblk):
        return pl.BlockSpec((nseq, T, D_GROUP), lambda b, t: (b, t, colblk))

    def full(shape):
        return pl.BlockSpec(shape, lambda b, t: (0,) * len(shape))

    st = pl.BlockSpec((nseq, ns_all), lambda b, t: (b, 0))
    y, s_re, s_im = pl.pallas_call(
        _s5_kernel,
        grid=(batch // nseq, nt),
        in_specs=[tok(COL_SU), tok(COL_SZ),
                  full(bd.shape), full(cd.shape), full(a_re.shape), full(a_im.shape),
                  full(dskip.shape), full(wglu.shape), full(bglu.shape), st, st],
        out_specs=[pl.BlockSpec((nseq, T, D_GROUP), lambda b, t: (b, t, 0)), st, st],
        out_shape=[jax.ShapeDtypeStruct((batch, seq, D_GROUP), BF16),
                   jax.ShapeDtypeStruct((batch, ns_all), F32),
                   jax.ShapeDtypeStruct((batch, ns_all), F32)],
        scratch_shapes=[pltpu.VMEM((D_GROUP // LANES, nseq * T, LANES), F32),
                        pltpu.VMEM((nseq * T, D_GROUP), F32),
                        pltpu.VMEM((nseq * T, ns_all), F32), pltpu.VMEM((nseq * T, ns_all), F32),
                        pltpu.VMEM((nseq, ns_all), F32), pltpu.VMEM((nseq, ns_all), F32)],
        compiler_params=_params(2),
        name="s5",
    )(u3, u3, bd, cd, a_re, a_im, dskip, wglu, bglu, s0_re, s0_im)
    return y.reshape(batch * seq, D_GROUP), s_re, s_im


def _rel_bias_table(rel_bias, n_q, n_k, offset, band):
    length = n_k + n_q - 1
    j = jnp.arange(length)
    idx = jnp.clip(offset + n_q - 1 - j, -REL_CLIP, REL_CLIP) + REL_CLIP
    ext = jnp.pad(rel_bias[..., idx].astype(F32), [(0, 0)] * (rel_bias.ndim - 1) + [(0, 1)])
    lead = ext.shape[:-1]
    flat = jnp.tile(ext, n_q)[..., :n_q * length]
    tbl = flat.reshape(lead + (n_q, length))[..., n_q - 1:n_q - 1 + n_k]
    if band:
        qc = jnp.arange(n_q)[:, None] // CHUNK
        kc = jnp.arange(n_k)[None, :] // CHUNK
        tbl = jnp.where((kc >= qc) & (kc <= qc + BAND), tbl, -jnp.inf)
    return tbl


def _s5_params(a_re, a_im, log_dt, b_re, b_im, c_re, c_im):
    dt = jnp.exp(log_dt)[:, None]
    mag = jnp.exp(a_re * dt)
    ab_re, ab_im = mag * jnp.cos(a_im * dt), mag * jnp.sin(a_im * dt)
    den = a_re * a_re + a_im * a_im
    co_re = ((ab_re - 1.0) * a_re + ab_im * a_im) / den
    co_im = (ab_im * a_re - (ab_re - 1.0) * a_im) / den
    bb_re = co_re[..., None] * b_re - co_im[..., None] * b_im
    bb_im = co_re[..., None] * b_im + co_im[..., None] * b_re
    a8_re = jnp.broadcast_to(ab_re.reshape(1, -1), (SUBLANES, S5_GROUPS * S5_STATE))
    a8_im = jnp.broadcast_to(ab_im.reshape(1, -1), (SUBLANES, S5_GROUPS * S5_STATE))
    gpb = S5_GROUPS // S5_BLOCKS
    eye = jnp.eye(gpb, dtype=F32)

    def blockdiag_in(bb):
        bbk = bb.reshape(S5_BLOCKS, gpb, S5_STATE, S5_CH)
        return jnp.einsum('kgpc,gh->kgchp', bbk, eye).reshape(S5_BLOCKS, gpb * S5_CH, gpb * S5_STATE)

    def blockdiag_out(cc):
        cck = cc.reshape(S5_BLOCKS, gpb, S5_CH, S5_STATE)
        return jnp.einsum('kgcp,gh->kgphc', cck, eye).reshape(S5_BLOCKS, gpb * S5_STATE, gpb * S5_CH)

    bd = jnp.concatenate([blockdiag_in(bb_re), blockdiag_in(bb_im)], axis=-1).astype(BF16)
    cd = jnp.concatenate([blockdiag_out(c_re), blockdiag_out(-c_im)], axis=1).astype(BF16)
    return bd, cd, a8_re, a8_im


def _tiles(batch, seq):
    return (min(batch * seq, IN_PROJ_ROWS), IN_PROJ_COLS, min(seq, OUT_PROJ_ROWS),
            min(seq, MLSTM_CHUNK), min(seq, S5_FRAMES))


def _layer(x, xn, batch, seq, att_cache, ml_state, conv_state, s5_state, p, layer, last, tiles):
    tm_in, tn_in, tm_out, L, t_s5 = tiles
    u, gates = _inproj(xn, p["w_head"], p["w_tail"], p["w_gate"], layer, tm_in, tn_in)
    nc = seq // L
    gates_t = gates[:, :SUBLANES].reshape(batch, nc, L, SUBLANES).transpose(0, 1, 3, 2)

    if att_cache is None:
        ya = _attn_prompt(u, p["bias"], batch, seq)
        rows = min(BAND * CHUNK, seq)
        u3 = u.reshape(batch, seq, N_MAIN)
        new_k = u3[:, seq - rows:, COL_AK * D_GROUP:(COL_AK + 1) * D_GROUP].astype(F32)
        new_v = u3[:, seq - rows:, COL_AV * D_GROUP:(COL_AV + 1) * D_GROUP].astype(F32)
    else:
        ya, new_k, new_v = _attn_sample(u, att_cache[0], att_cache[1], layer, p["bias"], batch, seq)
    new_k = new_k.reshape(batch, -1, ATT_HEADS, ATT_DH)
    new_v = new_v.reshape(batch, -1, ATT_HEADS, ATT_DH)

    c0, n0, m0 = ml_state
    m0 = jnp.broadcast_to(jnp.pad(m0, ((0, 0), (0, SUBLANES - ML_HEADS)))[:, :, None],
                          (batch, SUBLANES, ML_DH))
    ym, c_new, n_new, m_new = _mlstm(u, gates, gates_t, p["b_col"], p["b_row"], c0, n0, m0, batch, seq, L)
    m_new = m_new[:, :ML_HEADS, 0]

    ys, s5_re, s5_im = _s5(u, p["bd"], p["cd"], p["a_re"], p["a_im"], p["dskip"], p["wglu"], p["bglu"],
                           s5_state[0].reshape(batch, -1), s5_state[1].reshape(batch, -1),
                           batch, seq, t_s5)
    s5_re = s5_re.reshape(batch, S5_GROUPS, S5_STATE)
    s5_im = s5_im.reshape(batch, S5_GROUPS, S5_STATE)

    *outs, conv_new = _outproj(ya, ym, ys, u, p["conv_w"], conv_state, x, p["w_out_all"], layer, p["g_next"],
                               last, batch, seq, tm_out)
    return outs, (new_k, new_v, c_new, n_new, m_new, conv_new, s5_re, s5_im)


def kernel(x_prompt, x_sample, cache_attn_k, cache_attn_v, state_mlstm_C, state_mlstm_n, state_mlstm_m, state_conv, state_s5_re, state_s5_im, norm_g, w_in, w_out, attn_rel_bias, mlstm_b_if, conv_w, s5_A_re, s5_A_im, s5_log_dt, s5_B_re, s5_B_im, s5_C_re, s5_C_im, s5_D, s5_w_glu, s5_b_glu, final_norm_g):
    depth = w_in.shape[0]
    bp, sp, _ = x_prompt.shape
    bs, ss, _ = x_sample.shape
    w_rows = cache_attn_k.shape[2]
    gate_lo = 9 * D_GROUP
    gate_hi = gate_lo + 2 * ML_HEADS

    xp = x_prompt.reshape(bp * sp, D_MODEL)
    xs = x_sample.reshape(bs * ss, D_MODEL)
    ml0 = (jnp.zeros((bp, ML_HEADS, ML_DH, ML_DH), F32), jnp.zeros((bp, ML_HEADS, ML_DH), F32),
           jnp.zeros((bp, ML_HEADS), F32))
    conv0 = jnp.zeros((bp, CONV_W - 1, D_GROUP), F32)
    s50 = (jnp.zeros((bp, S5_GROUPS, S5_STATE), F32), jnp.zeros((bp, S5_GROUPS, S5_STATE), F32))

    tiles_p = _tiles(bp, sp)
    tiles_s = _tiles(bs, ss)
    att_rows = ATT_CHUNKS_PER_STEP * CHUNK
    bias_p = _rel_bias_table(attn_rel_bias, att_rows, BAND * CHUNK + att_rows, BAND * CHUNK, True)
    bias_s = _rel_bias_table(attn_rel_bias, ss, w_rows + ss, w_rows, False)
    w_head = jnp.swapaxes(w_in, 1, 2).astype(BF16)
    w_tail = w_head[:, gate_hi:]
    w_gate = jnp.pad(w_head[:, gate_lo:gate_hi], ((0, 0), (0, GATE_PAD - 2 * ML_HEADS), (0, 0)))
    w_out_all = w_out.astype(BF16)
    g_all = jnp.concatenate([norm_g, final_norm_g[None]], axis=0).reshape(depth + 1, 1, D_MODEL)
    xnp = _prenorm(xp, g_all[0], tiles_p[0])
    xns = _prenorm(xs, g_all[0], bs * ss)

    st_p, st_s = [], []
    for l in range(depth):
        bd, cd, a8_re, a8_im = _s5_params(s5_A_re[l], s5_A_im[l], s5_log_dt[l], s5_B_re[l], s5_B_im[l],
                                          s5_C_re[l], s5_C_im[l])
        b_if = mlstm_b_if[l]
        last = l == depth - 1
        p = {
            "w_head": w_head, "w_tail": w_tail, "w_gate": w_gate, "w_out_all": w_out_all, "g_next": g_all[l + 1],
            "b_col": jnp.pad(b_if, (0, GATE_PAD - 2 * ML_HEADS)).reshape(1, GATE_PAD),
            "b_row": b_if.reshape(SUBLANES, 1),
            "conv_w": conv_w[l],
            "bd": bd, "cd": cd, "a_re": a8_re, "a_im": a8_im,
            "dskip": s5_D[l].reshape(1, D_GROUP),
            "wglu": s5_w_glu[l].astype(BF16),
            "bglu": s5_b_glu[l].reshape(1, D_GROUP),
        }
        p_prompt = dict(p, bias=bias_p[l])
        p_sample = dict(p, bias=bias_s[l])

        out_p, sp_l = _layer(xp, xnp, bp, sp, None, ml0, conv0, s50, p_prompt, l, last, tiles_p)
        out_s, ss_l = _layer(xs, xns, bs, ss,
                             (cache_attn_k.reshape(depth, bs, w_rows, D_GROUP),
                              cache_attn_v.reshape(depth, bs, w_rows, D_GROUP)),
                             (state_mlstm_C[l], state_mlstm_n[l], state_mlstm_m[l]),
                             state_conv[l], (state_s5_re[l], state_s5_im[l]), p_sample, l, last, tiles_s)
        st_p.append(sp_l)
        st_s.append(ss_l)
        if not last:
            (xp, xnp), (xs, xns) = out_p, out_s

    y_prompt = out_p[0].reshape(bp, sp, D_MODEL)
    y_sample = out_s[0].reshape(bs, ss, D_MODEL)
    outs_p = [jnp.stack(t) for t in zip(*st_p)]
    outs_s = [jnp.stack(t) for t in zip(*st_s)]
    return (y_prompt, y_sample, *outs_p, *outs_s)
```

```python
import functools
import math

import jax
import jax.numpy as jnp
from jax import lax
from jax.experimental import pallas as pl
from jax.experimental.pallas import tpu as pltpu

F32 = jnp.float32
BF16 = jnp.bfloat16

D_MODEL = 2048
D_GROUP = 512
CHUNK = 64
BAND = 8
REL_CLIP = 128
ATT_HEADS = 8
ATT_DH = 64
ML_HEADS = 4
ML_DH = 128
CONV_W = 3
S5_GROUPS = 32
S5_CH = 16
S5_STATE = 64
S5_BLOCKS = 4
S5_BLK_STATES = S5_GROUPS * S5_STATE // S5_BLOCKS
NORM_EPS = 1e-6
N_MAIN = 15 * D_GROUP
N_HEAD = 9 * D_GROUP
GATE_PAD = 128
SUBLANES = 8
LANES = 128
IN_PROJ_ROWS = 1024
IN_PROJ_COLS = 3 * D_GROUP
OUT_PROJ_ROWS = 512
MLSTM_CHUNK = 4 * CHUNK
S5_FRAMES = 128
ATT_CHUNKS_PER_STEP = 4
VMEM_LIMIT = 56 * 1024 * 1024

COL_AQ, COL_AK, COL_AV, COL_AZ = 0, 1, 2, 3
COL_MQ, COL_MK, COL_MV, COL_MO, COL_MZ = 4, 5, 6, 7, 8
COL_CB, COL_CC, COL_CX, COL_CZ = 9, 10, 11, 12
COL_SU, COL_SZ = 13, 14


def _params(n_axes):
    return pltpu.CompilerParams(dimension_semantics=("arbitrary",) * n_axes,
                                vmem_limit_bytes=VMEM_LIMIT)


def _sigmoid(x):
    return 1.0 / (1.0 + jnp.exp(-x))


def _silu(x):
    return x * _sigmoid(x)


def _log_sigmoid(x):
    return jnp.minimum(x, 0.0) - jnp.log(1.0 + jnp.exp(-jnp.abs(x)))


def _gelu_tanh(x):
    return 0.5 * x * (1.0 + jnp.tanh(math.sqrt(2.0 / math.pi) * (x + 0.044715 * (x * x * x))))


def _rms(x, g):
    return x * lax.rsqrt(jnp.mean(x * x, axis=-1, keepdims=True) + NORM_EPS) * g


def _dot(a, b):
    return jnp.dot(a, b, preferred_element_type=F32)


def _dot_nt(a, b):
    return lax.dot_general(a, b, (((1,), (1,)), ((), ())), preferred_element_type=F32)


def _dot_exact(a, b):
    return jnp.dot(a, b, preferred_element_type=F32, precision=lax.Precision.HIGHEST)


def _prenorm_kernel(x_ref, g_ref, o_ref):
    o_ref[...] = _rms(x_ref[...], g_ref[...]).astype(o_ref.dtype)


def _prenorm(x, g, tm):
    n = x.shape[0]
    return pl.pallas_call(
        _prenorm_kernel,
        grid=(n // tm,),
        in_specs=[pl.BlockSpec((tm, D_MODEL), lambda i: (i, 0)),
                  pl.BlockSpec((1, D_MODEL), lambda i: (0, 0))],
        out_specs=pl.BlockSpec((tm, D_MODEL), lambda i: (i, 0)),
        out_shape=jax.ShapeDtypeStruct((n, D_MODEL), BF16),
        compiler_params=_params(1),
        name="prenorm",
    )(x, g)


def _inproj_kernel(xn_ref, wh_ref, wt_ref, wg_ref, u_ref, gate_ref, *, head_tiles):
    j = pl.program_id(1)

    @pl.when(j == 0)
    def _():
        gate_ref[...] = _dot_nt(xn_ref[...], wg_ref[...])

    @pl.when(j < head_tiles)
    def _():
        u_ref[...] = _dot_nt(xn_ref[...], wh_ref[...]).astype(u_ref.dtype)

    @pl.when(j >= head_tiles)
    def _():
        u_ref[...] = _dot_nt(xn_ref[...], wt_ref[...]).astype(u_ref.dtype)


def _inproj(xn, w_head, w_tail, w_gate, layer, tm, tn):
    n = xn.shape[0]
    head_tiles = N_HEAD // tn
    return pl.pallas_call(
        functools.partial(_inproj_kernel, head_tiles=head_tiles),
        grid=(n // tm, N_MAIN // tn),
        in_specs=[
            pl.BlockSpec((tm, D_MODEL), lambda i, j: (i, 0)),
            pl.BlockSpec((None, tn, D_MODEL), lambda i, j: (layer, jnp.minimum(j, head_tiles - 1), 0)),
            pl.BlockSpec((None, tn, D_MODEL), lambda i, j: (layer, jnp.maximum(j - head_tiles, 0), 0)),
            pl.BlockSpec((None, GATE_PAD, D_MODEL), lambda i, j: (layer, 0, 0)),
        ],
        out_specs=[
            pl.BlockSpec((tm, tn), lambda i, j: (i, j)),
            pl.BlockSpec((tm, GATE_PAD), lambda i, j: (i, 0)),
        ],
        out_shape=[
            jax.ShapeDtypeStruct((n, N_MAIN), BF16),
            jax.ShapeDtypeStruct((n, GATE_PAD), F32),
        ],
        compiler_params=_params(2),
        name="inproj",
    )(xn, w_head, w_tail, w_gate)


def _outproj_kernel(ya_ref, ym_ref, ys_ref, cb_ref, cc_ref, cx_ref, cz_ref, cw_ref, cs0_ref,
                    x_ref, w_ref, g_ref, *refs, tiles_per_seq):
    *out_refs, cs_ref, u_s = refs
    tm = x_ref.shape[0]
    head = SUBLANES
    taps = CONV_W - 1

    @pl.when(pl.program_id(0) % tiles_per_seq == 0)
    def _():
        u_s[head - taps:head, :] = cs0_ref[0]

    u_s[head:, :] = cc_ref[...].astype(F32) * cx_ref[...].astype(F32)
    y = cw_ref[0:1, :] * u_s[head - 2:head - 2 + tm, :]
    y = y + cw_ref[1:2, :] * u_s[head - 1:head - 1 + tm, :]
    y = y + cw_ref[2:3, :] * u_s[head:, :]
    yc = (_silu(cz_ref[...].astype(F32)) * (cb_ref[...].astype(F32) * y)).astype(BF16)
    last_rows = u_s[head + tm - taps:, :]
    cs_ref[0] = last_rows
    u_s[head - taps:head, :] = last_rows

    acc = x_ref[...] + _dot(jnp.concatenate([ya_ref[...], ym_ref[...], yc, ys_ref[...]], axis=1), w_ref[...])
    normed = _rms(acc, g_ref[...])
    if len(out_refs) == 2:
        out_refs[0][...] = acc
    out_refs[-1][...] = normed.astype(out_refs[-1].dtype)


def _outproj(ya, ym, ys, u, conv_w, conv_s0, x, w_out_all, layer, g_next, last, batch, seq, tm):
    n = x.shape[0]
    tps = seq // tm
    yspec = pl.BlockSpec((tm, D_GROUP), lambda i: (i, 0))
    row = pl.BlockSpec((tm, D_MODEL), lambda i: (i, 0))
    st = pl.BlockSpec((1, CONV_W - 1, D_GROUP), lambda i: (i // tps, 0, 0))

    def tok(colblk):
        return pl.BlockSpec((tm, D_GROUP), lambda i: (i, colblk))

    out_specs, out_shape = [row], [jax.ShapeDtypeStruct((n, D_MODEL), F32)]
    if not last:
        out_specs.append(row)
        out_shape.append(jax.ShapeDtypeStruct((n, D_MODEL), BF16))
    out_specs.append(st)
    out_shape.append(jax.ShapeDtypeStruct((batch, CONV_W - 1, D_GROUP), F32))
    return pl.pallas_call(
        functools.partial(_outproj_kernel, tiles_per_seq=tps),
        grid=(n // tm,),
        in_specs=[yspec, yspec, yspec, tok(COL_CB), tok(COL_CC), tok(COL_CX), tok(COL_CZ),
                  pl.BlockSpec((CONV_W, D_GROUP), lambda i: (0, 0)), st, row,
                  pl.BlockSpec((None, D_MODEL, D_MODEL), lambda i: (layer, 0, 0)),
                  pl.BlockSpec((1, D_MODEL), lambda i: (0, 0))],
        out_specs=out_specs,
        out_shape=out_shape,
        scratch_shapes=[pltpu.VMEM((SUBLANES + tm, D_GROUP), F32)],
        compiler_params=_params(1),
        name="outproj",
    )(ya, ym, ys, u, u, u, u, conv_w, conv_s0, x, w_out_all, g_next)


def _attn_heads(q, z, k_all, v_all, bias_ref, first_key, o_ref):
    lq, nk = q.shape[0], k_all.shape[0]
    pair_w = 2 * ATT_DH
    first = lax.broadcasted_iota(jnp.int32, (lq, pair_w), 1) < ATT_DH
    kcol = lax.broadcasted_iota(jnp.int32, (2 * lq, nk), 1)
    ones = jnp.ones((nk, pair_w), BF16)
    for pair in range(ATT_HEADS // 2):
        sl = slice(pair * pair_w, (pair + 1) * pair_w)
        qp = q[:, sl] * (ATT_DH ** -0.5)
        kp, vp = k_all[:, sl], v_all[:, sl]
        q2 = jnp.concatenate([jnp.where(first, qp, 0.0), jnp.where(first, 0.0, qp)], axis=0).astype(BF16)
        s = _dot_nt(q2, kp) + jnp.concatenate([bias_ref[2 * pair], bias_ref[2 * pair + 1]], axis=0)
        if first_key is not None:
            s = jnp.where(kcol >= first_key, s, -jnp.inf)
        p = jnp.exp(s - jnp.max(s, axis=-1, keepdims=True)).astype(BF16)
        ol = _dot(p, jnp.concatenate([vp, ones], axis=1))
        on = ol[:, :pair_w] * (1.0 / ol[:, pair_w:])
        o = jnp.where(first, on[:lq], on[lq:])
        o_ref[:, sl] = (_silu(z[:, sl]) * o).astype(o_ref.dtype)


def _attn_prompt_kernel(q_ref, k_ref, v_ref, z_ref, bias_ref, o_ref, kpad_ref, vpad_ref):
    step = pl.program_id(1)
    pad = BAND * CHUNK
    rows = q_ref.shape[0]
    nk = pad + rows

    @pl.when(step == 0)
    def _():
        kpad_ref[0:pad, :] = jnp.zeros((pad, D_GROUP), BF16)
        vpad_ref[0:pad, :] = jnp.zeros((pad, D_GROUP), BF16)
        kpad_ref[pad:, :] = k_ref[...].astype(BF16)
        vpad_ref[pad:, :] = v_ref[...].astype(BF16)

    start = pl.multiple_of(step * rows, rows)
    k_all = kpad_ref[pl.ds(start, nk), :]
    v_all = vpad_ref[pl.ds(start, nk), :]
    _attn_heads(q_ref[...].astype(F32), z_ref[...].astype(F32), k_all, v_all, bias_ref,
                pad - step * rows, o_ref)


def _attn_prompt(u, bias, batch, seq):
    rows, nk = bias.shape[1], bias.shape[2]
    ns = seq // rows
    return pl.pallas_call(
        _attn_prompt_kernel,
        grid=(batch, ns),
        in_specs=[
            pl.BlockSpec((rows, D_GROUP), lambda b, s: (b * ns + s, COL_AQ)),
            pl.BlockSpec((seq, D_GROUP), lambda b, s: (b, COL_AK)),
            pl.BlockSpec((seq, D_GROUP), lambda b, s: (b, COL_AV)),
            pl.BlockSpec((rows, D_GROUP), lambda b, s: (b * ns + s, COL_AZ)),
            pl.BlockSpec((ATT_HEADS, rows, nk), lambda b, s: (0, 0, 0)),
        ],
        out_specs=pl.BlockSpec((rows, D_GROUP), lambda b, s: (b * ns + s, 0)),
        out_shape=jax.ShapeDtypeStruct((batch * seq, D_GROUP), BF16),
        scratch_shapes=[pltpu.VMEM((seq + BAND * CHUNK, D_GROUP), BF16),
                        pltpu.VMEM((seq + BAND * CHUNK, D_GROUP), BF16)],
        compiler_params=_params(2),
        name="attn_prompt",
    )(u, u, u, u, bias)


def _attn_sample_kernel(q_ref, k_ref, v_ref, z_ref, kc_ref, vc_ref, bias_ref,
                        o_ref, kn_ref, vn_ref, kall_ref, vall_ref):
    w = kc_ref.shape[1]
    t = q_ref.shape[0]
    kall_ref[0:w, :] = kc_ref[0].astype(BF16)
    vall_ref[0:w, :] = vc_ref[0].astype(BF16)
    kall_ref[w:, :] = k_ref[...].astype(BF16)
    vall_ref[w:, :] = v_ref[...].astype(BF16)
    kn_ref[0, 0:w - t, :] = kc_ref[0, t:, :]
    vn_ref[0, 0:w - t, :] = vc_ref[0, t:, :]
    kn_ref[0, w - t:, :] = k_ref[...].astype(F32)
    vn_ref[0, w - t:, :] = v_ref[...].astype(F32)
    _attn_heads(q_ref[...].astype(F32), z_ref[...].astype(F32), kall_ref[...], vall_ref[...], bias_ref, None, o_ref)


def _attn_sample(u, k_cache_all, v_cache_all, layer, bias, batch, seq):
    w = k_cache_all.shape[2]
    tok = pl.BlockSpec((seq, D_GROUP), lambda b: (b, 0))
    cache = pl.BlockSpec((1, w, D_GROUP), lambda b: (b, 0, 0))
    cache_in = pl.BlockSpec((None, 1, w, D_GROUP), lambda b: (layer, b, 0, 0))
    return pl.pallas_call(
        _attn_sample_kernel,
        grid=(batch,),
        in_specs=[
            pl.BlockSpec((seq, D_GROUP), lambda b: (b, COL_AQ)),
            pl.BlockSpec((seq, D_GROUP), lambda b: (b, COL_AK)),
            pl.BlockSpec((seq, D_GROUP), lambda b: (b, COL_AV)),
            pl.BlockSpec((seq, D_GROUP), lambda b: (b, COL_AZ)),
            cache_in, cache_in,
            pl.BlockSpec((ATT_HEADS, seq, w + seq), lambda b: (0, 0, 0)),
        ],
        out_specs=[tok, cache, cache],
        out_shape=[jax.ShapeDtypeStruct((batch * seq, D_GROUP), BF16),
                   jax.ShapeDtypeStruct((batch, w, D_GROUP), F32),
                   jax.ShapeDtypeStruct((batch, w, D_GROUP), F32)],
        scratch_shapes=[pltpu.VMEM((w + seq, D_GROUP), BF16),
                        pltpu.VMEM((w + seq, D_GROUP), BF16)],
        compiler_params=_params(1),
        name="attn_sample",
    )(u, u, u, u, k_cache_all, v_cache_all, bias)


def _mlstm_kernel(q_ref, k_ref, v_ref, o_ref, z_ref, gc_ref, gr_ref, bc_ref, br_ref,
                  c0_ref, n0_ref, m0_ref,
                  y_ref, cout_ref, nout_ref, mout_ref, c_s, n_s, m_s):
    c = pl.program_id(1)
    L = q_ref.shape[0]

    @pl.when(c == 0)
    def _():
        c_s[...] = c0_ref[0]
        n_s[...] = n0_ref[0]
        m_s[...] = m0_ref[0]

    row = lax.broadcasted_iota(jnp.int32, (L, L), 0)
    col = lax.broadcasted_iota(jnp.int32, (L, L), 1)
    tril = (row >= col).astype(F32)
    triu = (row <= col).astype(F32)
    causal = row >= col
    ones_lv = jnp.ones((L, ML_DH), BF16)
    eye_bf = (lax.broadcasted_iota(jnp.int32, (ML_DH, ML_DH), 0)
              == lax.broadcasted_iota(jnp.int32, (ML_DH, ML_DH), 1)).astype(BF16)

    g_col = gc_ref[...] + bc_ref[...]
    g_row = gr_ref[0, 0] + br_ref[...]
    b_col = _dot_exact(tril, _log_sigmoid(g_col))
    b_row = _dot_exact(_log_sigmoid(g_row), triu)

    m_all, n_all = m_s[...], n_s[...]
    c_all = [c_s[h] for h in range(ML_HEADS)]
    c_new, n_new, m_new_rows = [], [], []

    def over_keys(x):
        return x[:, :L] if L <= ML_DH else jnp.concatenate([x] * (L // ML_DH), axis=1)

    for h in range(ML_HEADS):
        sl = slice(h * ML_DH, (h + 1) * ML_DH)
        bt = jnp.broadcast_to(b_col[:, ML_HEADS + h:ML_HEADS + h + 1], (L, ML_DH))
        ig_t = jnp.broadcast_to(g_col[:, h:h + 1], (L, ML_DH))
        bs = b_row[ML_HEADS + h:ML_HEADS + h + 1, :]
        ig_s = g_row[h:h + 1, :]
        m_prev = m_all[h:h + 1, :]
        n_prev = n_all[h:h + 1, :]
        c_prev = c_all[h]

        qh = q_ref[:, sl].astype(F32)
        kh = k_ref[:, sl].astype(F32) * (ML_DH ** -0.5)
        vh = v_ref[:, sl].astype(F32)
        qb, kb = qh.astype(BF16), kh.astype(BF16)

        d = jnp.where(causal, over_keys(bt) - bs + ig_s, -jnp.inf)
        m_inter = bt + m_prev
        m_t = jnp.maximum(m_inter, jnp.broadcast_to(jnp.max(d, axis=-1, keepdims=True), (L, ML_DH)))
        w_intra = jnp.exp(d - over_keys(m_t)) * _dot_nt(qb, kb)
        w_inter = jnp.exp(m_inter - m_t)
        pv = _dot(w_intra.astype(BF16), jnp.concatenate([vh.astype(BF16), ones_lv], axis=1))
        c_aug = jnp.concatenate([c_prev, jnp.broadcast_to(n_prev, (ML_DH, ML_DH))], axis=0).astype(BF16)
        cq = _dot_nt(qb, c_aug)
        num = w_inter * cq[:, :ML_DH] + pv[:, :ML_DH]
        den = w_inter * cq[:, ML_DH:] + pv[:, ML_DH:]
        hh = num / jnp.maximum(jnp.abs(den), jnp.exp(-m_t))
        ym = _sigmoid(o_ref[:, sl].astype(F32)) * hh
        y_ref[:, sl] = (_silu(z_ref[:, sl].astype(F32)) * ym).astype(y_ref.dtype)

        b_last = bt[L - 1:L, :]
        m_new = jnp.maximum(b_last + m_prev,
                            jnp.max(b_last[:, 0:1] - bs + ig_s, axis=-1, keepdims=True))
        ws = jnp.exp(b_last - bt + ig_t - m_new)
        decay = jnp.exp(b_last + m_prev - m_new)
        vw_t = _dot_nt(eye_bf, (vh * ws).astype(BF16)).astype(BF16)
        c_new.append(decay * c_prev + _dot(vw_t, kb))
        n_new.append(decay * n_prev + jnp.sum(ws * kh, axis=0, keepdims=True))
        m_new_rows.append(m_new)

    for h in range(ML_HEADS):
        c_s[h] = c_new[h]
        cout_ref[0, h] = c_new[h]
    n_cat = jnp.concatenate(n_new, axis=0)
    n_s[...] = n_cat
    nout_ref[0] = n_cat
    m_cat = jnp.concatenate(m_new_rows + [m_all[ML_HEADS:, :]], axis=0)
    m_s[...] = m_cat
    mout_ref[0] = m_cat


def _mlstm(u, gates, gates_t, b_col, b_row, c0, n0, m0, batch, seq, L):
    nc = seq // L

    def tok(colblk):
        return pl.BlockSpec((L, D_GROUP), lambda b, c: (b * nc + c, colblk))

    st_c = pl.BlockSpec((1, ML_HEADS, ML_DH, ML_DH), lambda b, c: (b, 0, 0, 0))
    st_n = pl.BlockSpec((1, ML_HEADS, ML_DH), lambda b, c: (b, 0, 0))
    st_m = pl.BlockSpec((1, SUBLANES, ML_DH), lambda b, c: (b, 0, 0))
    return pl.pallas_call(
        _mlstm_kernel,
        grid=(batch, nc),
        in_specs=[
            tok(COL_MQ), tok(COL_MK), tok(COL_MV), tok(COL_MO), tok(COL_MZ),
            pl.BlockSpec((L, GATE_PAD), lambda b, c: (b * nc + c, 0)),
            pl.BlockSpec((1, 1, SUBLANES, L), lambda b, c: (b, c, 0, 0)),
            pl.BlockSpec((1, GATE_PAD), lambda b, c: (0, 0)),
            pl.BlockSpec((SUBLANES, 1), lambda b, c: (0, 0)),
            st_c, st_n, st_m,
        ],
        out_specs=[pl.BlockSpec((L, D_GROUP), lambda b, c: (b * nc + c, 0)), st_c, st_n, st_m],
        out_shape=[jax.ShapeDtypeStruct((batch * seq, D_GROUP), BF16),
                   jax.ShapeDtypeStruct((batch, ML_HEADS, ML_DH, ML_DH), F32),
                   jax.ShapeDtypeStruct((batch, ML_HEADS, ML_DH), F32),
                   jax.ShapeDtypeStruct((batch, SUBLANES, ML_DH), F32)],
        scratch_shapes=[pltpu.VMEM((ML_HEADS, ML_DH, ML_DH), F32),
                        pltpu.VMEM((ML_HEADS, ML_DH), F32),
                        pltpu.VMEM((SUBLANES, ML_DH), F32)],
        compiler_params=_params(2),
        name="mlstm",
    )(u, u, u, u, u, gates, gates_t, b_col, b_row, c0, n0, m0)


def _s5_kernel(u_ref, z_ref, bd_ref, cd_ref, are_ref, aim_ref, dskip_ref, wglu_ref, bglu_ref,
               s0re_ref, s0im_ref, y_ref, sre_ref, sim_ref, perm_s, up_s, xr_s, xi_s, car_s, cai_s):
    t = pl.program_id(1)
    nseq, T = u_ref.shape[0], u_ref.shape[1]
    rows = nseq * T
    NS = S5_BLK_STATES
    ntile = D_GROUP // LANES

    @pl.when(t == 0)
    def _():
        car_s[...] = s0re_ref[...]
        cai_s[...] = s0im_ref[...]

    u_seq = u_ref[...].astype(F32)
    up_s[...] = pltpu.einshape("igc->gic", u_seq).reshape(rows, D_GROUP)
    u = up_s[...]
    ub = u.astype(BF16)
    for blk in range(S5_BLOCKS):
        cols = slice(blk * NS, (blk + 1) * NS)
        bu = _dot(ub[:, blk * 128:(blk + 1) * 128], bd_ref[blk])
        xr_s[:, cols] = bu[:, :NS]
        xi_s[:, cols] = bu[:, NS:]

    span = 2 * NS
    for c0 in range(0, S5_BLOCKS * NS, span):
        cols = slice(c0, c0 + span)
        ar = are_ref[:, cols]
        ai = aim_ref[:, cols]

        def body(g, carry, cols=cols, ar=ar, ai=ai):
            xr, xi = carry
            frame = pl.ds(pl.multiple_of(g * nseq, nseq), nseq)
            nr = ar * xr - ai * xi + xr_s[frame, cols]
            ni = ar * xi + ai * xr + xi_s[frame, cols]
            xr_s[frame, cols] = nr
            xi_s[frame, cols] = ni
            return nr, ni

        cr, ci = lax.fori_loop(0, T, body, (car_s[:, cols], cai_s[:, cols]), unroll=4)
        car_s[:, cols] = cr
        cai_s[:, cols] = ci

    ys = []
    for blk in range(S5_BLOCKS):
        cols = slice(blk * NS, (blk + 1) * NS)
        ys.append(_dot(xr_s[:, cols].astype(BF16), cd_ref[blk, :NS, :])
                  + _dot(xi_s[:, cols].astype(BF16), cd_ref[blk, NS:, :]))
    y = _gelu_tanh(jnp.concatenate(ys, axis=1) + dskip_ref[...] * u)
    y = y * _sigmoid(_dot(y.astype(BF16), wglu_ref[...]) + bglu_ref[...])
    for j in range(ntile):
        perm_s[j] = y[:, j * LANES:(j + 1) * LANES]
    for i in range(nseq):
        y_i = jnp.concatenate([perm_s[j, pl.ds(i, T, stride=nseq), :] for j in range(ntile)], axis=1)
        y_ref[i] = (_silu(z_ref[i].astype(F32)) * y_i).astype(y_ref.dtype)
    sre_ref[...] = car_s[...]
    sim_ref[...] = cai_s[...]


def _s5(u, bd, cd, a_re, a_im, dskip, wglu, bglu, s0_re, s0_im, batch, seq, T):
    nseq = SUBLANES
    nt = seq // T
    ns_all = S5_GROUPS * S5_STATE
    u3 = u.reshape(batch, seq, N_MAIN)

    def tok(colblk):
        return pl.BlockSpec((nseq, T, D_GROUP), lambda b, t: (b, t, colblk))

    def full(shape):
        return pl.BlockSpec(shape, lambda b, t: (0,) * len(shape))

    st = pl.BlockSpec((nseq, ns_all), lambda b, t: (b, 0))
    y, s_re, s_im = pl.pallas_call(
        _s5_kernel,
        grid=(batch // nseq, nt),
        in_specs=[tok(COL_SU), tok(COL_SZ),
                  full(bd.shape), full(cd.shape), full(a_re.shape), full(a_im.shape),
                  full(dskip.shape), full(wglu.shape), full(bglu.shape), st, st],
        out_specs=[pl.BlockSpec((nseq, T, D_GROUP), lambda b, t: (b, t, 0)), st, st],
        out_shape=[jax.ShapeDtypeStruct((batch, seq, D_GROUP), BF16),
                   jax.ShapeDtypeStruct((batch, ns_all), F32),
                   jax.ShapeDtypeStruct((batch, ns_all), F32)],
        scratch_shapes=[pltpu.VMEM((D_GROUP // LANES, nseq * T, LANES), F32),
                        pltpu.VMEM((nseq * T, D_GROUP), F32),
                        pltpu.VMEM((nseq * T, ns_all), F32), pltpu.VMEM((nseq * T, ns_all), F32),
                        pltpu.VMEM((nseq, ns_all), F32), pltpu.VMEM((nseq, ns_all), F32)],
        compiler_params=_params(2),
        name="s5",
    )(u3, u3, bd, cd, a_re, a_im, dskip, wglu, bglu, s0_re, s0_im)
    return y.reshape(batch * seq, D_GROUP), s_re, s_im


def _rel_bias_table(rel_bias, n_q, n_k, offset, band):
    length = n_k + n_q - 1
    j = jnp.arange(length)
    idx = jnp.clip(offset + n_q - 1 - j, -REL_CLIP, REL_CLIP) + REL_CLIP
    ext = jnp.pad(rel_bias[..., idx].astype(F32), [(0, 0)] * (rel_bias.ndim - 1) + [(0, 1)])
    lead = ext.shape[:-1]
    flat = jnp.tile(ext, n_q)[..., :n_q * length]
    tbl = flat.reshape(lead + (n_q, length))[..., n_q - 1:n_q - 1 + n_k]
    if band:
        qc = jnp.arange(n_q)[:, None] // CHUNK
        kc = jnp.arange(n_k)[None, :] // CHUNK
        tbl = jnp.where((kc >= qc) & (kc <= qc + BAND), tbl, -jnp.inf)
    return tbl


def _s5_params(a_re, a_im, log_dt, b_re, b_im, c_re, c_im):
    dt = jnp.exp(log_dt)[:, None]
    mag = jnp.exp(a_re * dt)
    ab_re, ab_im = mag * jnp.cos(a_im * dt), mag * jnp.sin(a_im * dt)
    den = a_re * a_re + a_im * a_im
    co_re = ((ab_re - 1.0) * a_re + ab_im * a_im) / den
    co_im = (ab_im * a_re - (ab_re - 1.0) * a_im) / den
    bb_re = co_re[..., None] * b_re - co_im[..., None] * b_im
    bb_im = co_re[..., None] * b_im + co_im[..., None] * b_re
    a8_re = jnp.broadcast_to(ab_re.reshape(1, -1), (SUBLANES, S5_GROUPS * S5_STATE))
    a8_im = jnp.broadcast_to(ab_im.reshape(1, -1), (SUBLANES, S5_GROUPS * S5_STATE))
    gpb = S5_GROUPS // S5_BLOCKS
    eye = jnp.eye(gpb, dtype=F32)

    def blockdiag_in(bb):
        bbk = bb.reshape(S5_BLOCKS, gpb, S5_STATE, S5_CH)
        return jnp.einsum('kgpc,gh->kgchp', bbk, eye).reshape(S5_BLOCKS, gpb * S5_CH, gpb * S5_STATE)

    def blockdiag_out(cc):
        cck = cc.reshape(S5_BLOCKS, gpb, S5_CH, S5_STATE)
        return jnp.einsum('kgcp,gh->kgphc', cck, eye).reshape(S5_BLOCKS, gpb * S5_STATE, gpb * S5_CH)

    bd = jnp.concatenate([blockdiag_in(bb_re), blockdiag_in(bb_im)], axis=-1).astype(BF16)
    cd = jnp.concatenate([blockdiag_out(c_re), blockdiag_out(-c_im)], axis=1).astype(BF16)
    return bd, cd, a8_re, a8_im


def _tiles(batch, seq):
    return (min(batch * seq, IN_PROJ_ROWS), IN_PROJ_COLS, min(seq, OUT_PROJ_ROWS),
            min(seq, MLSTM_CHUNK), min(seq, S5_FRAMES))


def _layer(x, xn, batch, seq, att_cache, ml_state, conv_state, s5_state, p, layer, last, tiles):
    tm_in, tn_in, tm_out, L, t_s5 = tiles
    u, gates = _inproj(xn, p["w_head"], p["w_tail"], p["w_gate"], layer, tm_in, tn_in)
    nc = seq // L
    gates_t = gates[:, :SUBLANES].reshape(batch, nc, L, SUBLANES).transpose(0, 1, 3, 2)

    if att_cache is None:
        ya = _attn_prompt(u, p["bias"], batch, seq)
        rows = min(BAND * CHUNK, seq)
        u3 = u.reshape(batch, seq, N_MAIN)
        new_k = u3[:, seq - rows:, COL_AK * D_GROUP:(COL_AK + 1) * D_GROUP].astype(F32)
        new_v = u3[:, seq - rows:, COL_AV * D_GROUP:(COL_AV + 1) * D_GROUP].astype(F32)
    else:
        ya, new_k, new_v = _attn_sample(u, att_cache[0], att_cache[1], layer, p["bias"], batch, seq)
    new_k = new_k.reshape(batch, -1, ATT_HEADS, ATT_DH)
    new_v = new_v.reshape(batch, -1, ATT_HEADS, ATT_DH)

    c0, n0, m0 = ml_state
    m0 = jnp.broadcast_to(jnp.pad(m0, ((0, 0), (0, SUBLANES - ML_HEADS)))[:, :, None],
                          (batch, SUBLANES, ML_DH))
    ym, c_new, n_new, m_new = _mlstm(u, gates, gates_t, p["b_col"], p["b_row"], c0, n0, m0, batch, seq, L)
    m_new = m_new[:, :ML_HEADS, 0]

    ys, s5_re, s5_im = _s5(u, p["bd"], p["cd"], p["a_re"], p["a_im"], p["dskip"], p["wglu"], p["bglu"],
                           s5_state[0].reshape(batch, -1), s5_state[1].reshape(batch, -1),
                           batch, seq, t_s5)
    s5_re = s5_re.reshape(batch, S5_GROUPS, S5_STATE)
    s5_im = s5_im.reshape(batch, S5_GROUPS, S5_STATE)

    *outs, conv_new = _outproj(ya, ym, ys, u, p["conv_w"], conv_state, x, p["w_out_all"], layer, p["g_next"],
                               last, batch, seq, tm_out)
    return outs, (new_k, new_v, c_new, n_new, m_new, conv_new, s5_re, s5_im)


def kernel(x_prompt, x_sample, cache_attn_k, cache_attn_v, state_mlstm_C, state_mlstm_n, state_mlstm_m, state_conv, state_s5_re, state_s5_im, norm_g, w_in, w_out, attn_rel_bias, mlstm_b_if, conv_w, s5_A_re, s5_A_im, s5_log_dt, s5_B_re, s5_B_im, s5_C_re, s5_C_im, s5_D, s5_w_glu, s5_b_glu, final_norm_g):
    depth = w_in.shape[0]
    bp, sp, _ = x_prompt.shape
    bs, ss, _ = x_sample.shape
    w_rows = cache_attn_k.shape[2]
    gate_lo = 9 * D_GROUP
    gate_hi = gate_lo + 2 * ML_HEADS

    xp = x_prompt.reshape(bp * sp, D_MODEL)
    xs = x_sample.reshape(bs * ss, D_MODEL)
    ml0 = (jnp.zeros((bp, ML_HEADS, ML_DH, ML_DH), F32), jnp.zeros((bp, ML_HEADS, ML_DH), F32),
           jnp.zeros((bp, ML_HEADS), F32))
    conv0 = jnp.zeros((bp, CONV_W - 1, D_GROUP), F32)
    s50 = (jnp.zeros((bp, S5_GROUPS, S5_STATE), F32), jnp.zeros((bp, S5_GROUPS, S5_STATE), F32))

    tiles_p = _tiles(bp, sp)
    tiles_s = _tiles(bs, ss)
    att_rows = ATT_CHUNKS_PER_STEP * CHUNK
    bias_p = _rel_bias_table(attn_rel_bias, att_rows, BAND * CHUNK + att_rows, BAND * CHUNK, True)
    bias_s = _rel_bias_table(attn_rel_bias, ss, w_rows + ss, w_rows, False)
    w_head = jnp.swapaxes(w_in, 1, 2).astype(BF16)
    w_tail = w_head[:, gate_hi:]
    w_gate = jnp.pad(w_head[:, gate_lo:gate_hi], ((0, 0), (0, GATE_PAD - 2 * ML_HEADS), (0, 0)))
    w_out_all = w_out.astype(BF16)
    g_all = jnp.concatenate([norm_g, final_norm_g[None]], axis=0).reshape(depth + 1, 1, D_MODEL)
    xnp = _prenorm(xp, g_all[0], tiles_p[0])
    xns = _prenorm(xs, g_all[0], bs * ss)

    st_p, st_s = [], []
    for l in range(depth):
        bd, cd, a8_re, a8_im = _s5_params(s5_A_re[l], s5_A_im[l], s5_log_dt[l], s5_B_re[l], s5_B_im[l],
                                          s5_C_re[l], s5_C_im[l])
        b_if = mlstm_b_if[l]
        last = l == depth - 1
        p = {
            "w_head": w_head, "w_tail": w_tail, "w_gate": w_gate, "w_out_all": w_out_all, "g_next": g_all[l + 1],
            "b_col": jnp.pad(b_if, (0, GATE_PAD - 2 * ML_HEADS)).reshape(1, GATE_PAD),
            "b_row": b_if.reshape(SUBLANES, 1),
            "conv_w": conv_w[l],
            "bd": bd, "cd": cd, "a_re": a8_re, "a_im": a8_im,
            "dskip": s5_D[l].reshape(1, D_GROUP),
            "wglu": s5_w_glu[l].astype(BF16),
            "bglu": s5_b_glu[l].reshape(1, D_GROUP),
        }
        p_prompt = dict(p, bias=bias_p[l])
        p_sample = dict(p, bias=bias_s[l])

        out_p, sp_l = _layer(xp, xnp, bp, sp, None, ml0, conv0, s50, p_prompt, l, last, tiles_p)
        out_s, ss_l = _layer(xs, xns, bs, ss,
                             (cache_attn_k.reshape(depth, bs, w_rows, D_GROUP),
                              cache_attn_v.reshape(depth, bs, w_rows, D_GROUP)),
                             (state_mlstm_C[l], state_mlstm_n[l], state_mlstm_m[l]),
                             state_conv[l], (state_s5_re[l], state_s5_im[l]), p_sample, l, last, tiles_s)
        st_p.append(sp_l)
        st_s.append(ss_l)
        if not last:
            (xp, xnp), (xs, xns) = out_p, out_s

    y_prompt = out_p[0].reshape(bp, sp, D_MODEL)
    y_sample = out_s[0].reshape(bs, ss, D_MODEL)
    outs_p = [jnp.stack(t) for t in zip(*st_p)]
    outs_s = [jnp.stack(t) for t in zip(*st_s)]
    return (y_prompt, y_sample, *outs_p, *outs_s)
```
